```python
import math
import jax
import jax.numpy as jnp
from jax import lax
import numpy as np

D_MODEL = 4096
BATCH = 4
SEQ = 2048
DEPTH = 1
DEC_BATCH = 128
DEC_SEQ = 1
PAST_LEN = 2048
PAGE_SIZE = 128

HEAD_DIM = 128
MIX_WIDTH = D_MODEL
ATTN_HEADS = MIX_WIDTH // (2 * HEAD_DIM)
KV_HEADS = max(1, ATTN_HEADS // 4)
GDN_HEADS = MIX_WIDTH // (2 * HEAD_DIM)
GDN_DK = HEAD_DIM
GDN_DV = HEAD_DIM
CONV_W = 4
GDN_CHUNK = 64
MOBA_BLOCK = 256
MOBA_TOPK = 3
MOBA_Q_CHUNK = 32
ROPE_THETA = 500000.0
ROT_DIM = HEAD_DIM // 4
N_EXPERTS = 128
D_EXPERT = 512
D_SHARED = 512
TOP_K = 8
N_GROUPS = 8
TOPK_GROUPS = 4
ROUTED_SCALE = 2.5
MOE_MAX_ROWS = 128
NORM_EPS = 1e-6

Q_ATTN = ATTN_HEADS * HEAD_DIM
KV_ATTN = KV_HEADS * HEAD_DIM
QK_GDN = GDN_HEADS * GDN_DK
V_GDN = GDN_HEADS * GDN_DV
GDN_CONV_CH = 2 * QK_GDN + V_GDN
D_IN_PROJ = Q_ATTN + 2 * KV_ATTN + GDN_CONV_CH + V_GDN + 2 * GDN_HEADS
PROJ_SPLITS = [Q_ATTN, Q_ATTN + KV_ATTN, Q_ATTN + 2 * KV_ATTN, Q_ATTN + 2 * KV_ATTN + GDN_CONV_CH, Q_ATTN + 2 * KV_ATTN + GDN_CONV_CH + V_GDN, Q_ATTN + 2 * KV_ATTN + GDN_CONV_CH + V_GDN + GDN_HEADS]

kernel_name = 'hymba_moba_gdn_moe_decoder_step'


def rmsnorm(x, g):
    xf = x.astype(jnp.float32)
    y = xf * lax.rsqrt(jnp.mean(xf * xf, axis=-1, keepdims=True) + NORM_EPS)
    return (y * g.astype(jnp.float32)).astype(x.dtype)


def l2norm(x):
    return x * lax.rsqrt(jnp.sum(x * x, axis=-1, keepdims=True) + 1e-6)


def rope_partial(x, pos):
    half = ROT_DIM // 2
    inv = jnp.power(ROPE_THETA, -jnp.arange(half, dtype=jnp.float32) / half)
    ang = pos.astype(jnp.float32)[:, None] * inv[None, :]
    cos = jnp.cos(ang)[None, :, None, :]
    sin = jnp.sin(ang)[None, :, None, :]
    xr = x[..., :ROT_DIM].astype(jnp.float32)
    x1, x2 = xr[..., :half], xr[..., half:]
    rot = jnp.concatenate([x1 * cos - x2 * sin, x2 * cos + x1 * sin], axis=-1).astype(x.dtype)
    return jnp.concatenate([rot, x[..., ROT_DIM:]], axis=-1)


def moba_attention(q, k, v, q_pos):
    B, Sq = q.shape[0], q.shape[1]
    T = k.shape[1]
    nb = -(-T // MOBA_BLOCK)
    pad = nb * MOBA_BLOCK - T
    k = jnp.pad(k, ((0, 0), (0, pad), (0, 0), (0, 0)))
    v = jnp.pad(v, ((0, 0), (0, pad), (0, 0), (0, 0)))
    kb = k.reshape(B, nb, MOBA_BLOCK, KV_HEADS, HEAD_DIM).transpose(0, 3, 1, 2, 4)
    vb = v.reshape(B, nb, MOBA_BLOCK, KV_HEADS, HEAD_DIM).transpose(0, 3, 1, 2, 4)
    group = ATTN_HEADS // KV_HEADS
    kmean = jnp.mean(kb.astype(jnp.float32), axis=3)
    qg = q.astype(jnp.float32).reshape(B, Sq, KV_HEADS, group, HEAD_DIM)
    gate = jnp.einsum('bsgrd,bgnd->bsgrn', qg, kmean).reshape(B, Sq, ATTN_HEADS, nb)
    own = (q_pos // MOBA_BLOCK).astype(jnp.int32)
    past = jnp.arange(nb, dtype=jnp.int32)[None, :] < own[:, None]
    gate = jnp.where(past[None, :, None, :], gate, -jnp.inf)
    n_sel = min(MOBA_TOPK, nb)
    _, sel = lax.top_k(gate, n_sel)
    sel = sel.astype(jnp.int32)
    sel_valid = sel < own[None, :, None, None]
    blocks = jnp.concatenate([sel, jnp.broadcast_to(own[None, :, None, None], (B, Sq, ATTN_HEADS, 1))], axis=-1)
    valid = jnp.concatenate([sel_valid, jnp.ones((B, Sq, ATTN_HEADS, 1), bool)], axis=-1)
    qc = math.gcd(Sq, MOBA_Q_CHUNK)
    nc = Sq // qc
    n_items = B * nc

    def split(a):
        return a.reshape((n_items, qc) + a.shape[2:])

    b_idx = jnp.repeat(jnp.arange(B, dtype=jnp.int32), nc)
    pos_c = jnp.broadcast_to(q_pos.reshape(1, nc, qc), (B, nc, qc)).reshape(n_items, qc)
    kvh = jnp.arange(ATTN_HEADS, dtype=jnp.int32) // group
    scale = HEAD_DIM ** -0.5

    def one(args):
        bi, qi, bl, va, pi = args
        kg = kb[bi][kvh[None, :, None], bl].astype(jnp.float32)
        vg = vb[bi][kvh[None, :, None], bl].astype(jnp.float32)
        s = jnp.einsum('qhd,qhsjd->qhsj', qi.astype(jnp.float32), kg) * scale
        kpos = bl[..., None] * MOBA_BLOCK + jnp.arange(MOBA_BLOCK, dtype=jnp.int32)
        mask = va[..., None] & (kpos <= pi[:, None, None, None])
        s = jnp.where(mask, s, -jnp.inf)
        p = jax.nn.softmax(s.reshape(qc, ATTN_HEADS, -1), axis=-1).reshape(s.shape)
        return jnp.einsum('qhsj,qhsjd->qhd', p, vg).astype(q.dtype)

    out = lax.map(one, (b_idx, split(q), split(blocks), split(valid), pos_c))
    return out.reshape(B, Sq, ATTN_HEADS, HEAD_DIM)


def causal_conv(x, w, buf):
    S = x.shape[1]
    xp = jnp.concatenate([buf.astype(x.dtype), x], axis=1)
    y = xp[:, 0:S] * w[0]
    for i in range(1, CONV_W):
        y = y + xp[:, i:i + S] * w[i]
    return jax.nn.silu(y), xp[:, -(CONV_W - 1):]


def gdn_chunked(q, k, v, g, beta, s0):
    B, S, H = q.shape[0], q.shape[1], q.shape[2]
    C = GDN_CHUNK
    N = S // C

    def to_chunks(t):
        return t.reshape((B, N, C, H) + t.shape[3:]).swapaxes(2, 3)

    q, k, v, g, beta = to_chunks(q), to_chunks(k), to_chunks(v), to_chunks(g), to_chunks(beta)
    gc = jnp.cumsum(g, axis=-1)
    tril = jnp.tril(jnp.ones((C, C), bool))
    strict = jnp.tril(jnp.ones((C, C), bool), -1)
    decay = jnp.exp(jnp.where(tril, gc[..., :, None] - gc[..., None, :], -jnp.inf))
    kbeta = k * beta[..., None]
    vbeta = v * beta[..., None]
    lower = jnp.where(strict, jnp.einsum('bnhid,bnhjd->bnhij', kbeta, k) * decay, 0.0)
    eye = jnp.eye(C, dtype=jnp.float32)
    tmat = lax.linalg.triangular_solve(lower + eye, jnp.broadcast_to(eye, lower.shape), left_side=True, lower=True, unit_diagonal=True)
    u = jnp.einsum('bnhij,bnhjd->bnhid', tmat, vbeta)
    w = jnp.einsum('bnhij,bnhjd->bnhid', tmat, kbeta * jnp.exp(gc)[..., None])
    attn = jnp.where(tril, jnp.einsum('bnhid,bnhjd->bnhij', q, k) * decay, 0.0)
    g_last = gc[..., -1]
    k_tail = k * jnp.exp(g_last[..., None] - gc)[..., None]
    q_dec = q * jnp.exp(gc)[..., None]

    def step(state, xs):
        u_i, w_i, qd_i, kt_i, a_i, gl_i = xs
        v_new = u_i - jnp.einsum('bhcd,bhde->bhce', w_i, state)
        o = jnp.einsum('bhcd,bhde->bhce', qd_i, state) + jnp.einsum('bhij,bhje->bhie', a_i, v_new)
        state = state * jnp.exp(gl_i)[..., None, None] + jnp.einsum('bhcd,bhce->bhde', kt_i, v_new)
        return state, o

    xs = (u.swapaxes(0, 1), w.swapaxes(0, 1), q_dec.swapaxes(0, 1), k_tail.swapaxes(0, 1), attn.swapaxes(0, 1), g_last.swapaxes(0, 1))
    s_fin, o = lax.scan(step, s0, xs)
    o = o.transpose(1, 0, 3, 2, 4).reshape(B, S, H, GDN_DV)
    return o, s_fin


def gdn_recurrent(q, k, v, g, beta, s0):
    def step(state, xs):
        q_t, k_t, v_t, g_t, b_t = xs
        state = state * jnp.exp(g_t)[..., None, None]
        kv = jnp.einsum('bhd,bhde->bhe', k_t, state)
        delta = (v_t - kv) * b_t[..., None]
        state = state + k_t[..., :, None] * delta[..., None, :]
        return state, jnp.einsum('bhd,bhde->bhe', q_t, state)

    xs = (q.swapaxes(0, 1), k.swapaxes(0, 1), v.swapaxes(0, 1), g.swapaxes(0, 1), beta.swapaxes(0, 1))
    s_fin, o = lax.scan(step, s0, xs)
    return o.swapaxes(0, 1), s_fin


def token_mixer(h, pos, past_k, past_v, conv_buf, ssm0, chunked, w_in, conv_w, a_log, dt_bias, gdn_norm, w_out):
    B, S = h.shape[0], h.shape[1]
    proj = h @ w_in
    q_a, k_a, v_a, qkv_g, z_g, a_g, b_g = jnp.split(proj, PROJ_SPLITS, axis=-1)
    q_a = rope_partial(q_a.reshape(B, S, ATTN_HEADS, HEAD_DIM), pos)
    k_a = rope_partial(k_a.reshape(B, S, KV_HEADS, HEAD_DIM), pos)
    v_a = v_a.reshape(B, S, KV_HEADS, HEAD_DIM)
    if past_k is None:
        k_all, v_all = k_a, v_a
    else:
        k_all = jnp.concatenate([past_k.astype(k_a.dtype), k_a], axis=1)
        v_all = jnp.concatenate([past_v.astype(v_a.dtype), v_a], axis=1)
    o_a = moba_attention(q_a, k_all, v_all, pos)
    conv_out, conv_new = causal_conv(qkv_g, conv_w, conv_buf)
    q_g, k_g, v_g = jnp.split(conv_out.astype(jnp.float32), [QK_GDN, 2 * QK_GDN], axis=-1)
    q_g = l2norm(q_g.reshape(B, S, GDN_HEADS, GDN_DK)) * (GDN_DK ** -0.5)
    k_g = l2norm(k_g.reshape(B, S, GDN_HEADS, GDN_DK))
    v_g = v_g.reshape(B, S, GDN_HEADS, GDN_DV)
    beta = jax.nn.sigmoid(b_g.astype(jnp.float32))
    g = -jnp.exp(a_log.astype(jnp.float32)) * jax.nn.softplus(a_g.astype(jnp.float32) + dt_bias.astype(jnp.float32))
    if chunked:
        o_g, ssm_new = gdn_chunked(q_g, k_g, v_g, g, beta, ssm0)
    else:
        o_g, ssm_new = gdn_recurrent(q_g, k_g, v_g, g, beta, ssm0)
    z = z_g.reshape(B, S, GDN_HEADS, GDN_DV).astype(jnp.float32)
    o_g = rmsnorm(o_g, gdn_norm) * jax.nn.silu(z)
    mixed = jnp.concatenate([o_a.reshape(B, S, Q_ATTN), o_g.reshape(B, S, V_GDN).astype(h.dtype)], axis=-1)
    return mixed @ w_out, k_a, v_a, conv_new, ssm_new


def moe_ffn(h, l, w_router, e_bias, w1, w3, w2, ws1, ws3, ws2):
    N, D = h.shape
    scores = jax.nn.sigmoid(h.astype(jnp.float32) @ w_router[l].astype(jnp.float32))
    biased = scores + e_bias[l].astype(jnp.float32)
    grp_score = jnp.sum(lax.top_k(biased.reshape(N, N_GROUPS, N_EXPERTS // N_GROUPS), 2)[0], axis=-1)
    _, top_g = lax.top_k(grp_score, TOPK_GROUPS)
    gmask = jnp.sum(jax.nn.one_hot(top_g, N_GROUPS, dtype=jnp.float32), axis=1) > 0
    masked = jnp.where(jnp.repeat(gmask, N_EXPERTS // N_GROUPS, axis=1), biased, -jnp.inf)
    _, idx = lax.top_k(masked, TOP_K)
    wts = jnp.take_along_axis(scores, idx, axis=1)
    wts = wts / jnp.sum(wts, axis=-1, keepdims=True) * ROUTED_SCALE
    A = N * TOP_K
    mb = max(8, min(MOE_MAX_ROWS, A // N_EXPERTS))
    n_blocks = -(-A // mb) + N_EXPERTS
    flat_e = idx.reshape(-1).astype(jnp.int32)
    flat_t = jnp.repeat(jnp.arange(N, dtype=jnp.int32), TOP_K)
    flat_w = wts.reshape(-1)
    order = jnp.argsort(flat_e)
    se, st, sw = flat_e[order], flat_t[order], flat_w[order]
    counts = jnp.zeros((N_EXPERTS,), jnp.int32).at[flat_e].add(1)
    pcounts = (counts + mb - 1) // mb * mb
    start = jnp.cumsum(counts) - counts
    pend = jnp.cumsum(pcounts)
    pstart = pend - pcounts
    dest = pstart[se] + jnp.arange(A, dtype=jnp.int32) - start[se]
    rows = n_blocks * mb
    row_tok = jnp.full((rows,), N, jnp.int32).at[dest].set(st)
    row_w = jnp.zeros((rows,), jnp.float32).at[dest].set(sw)
    block_e = jnp.minimum(jnp.searchsorted(pend, jnp.arange(n_blocks, dtype=jnp.int32) * mb, side='right'), N_EXPERTS - 1)
    h_pad = jnp.concatenate([h, jnp.zeros((1, D), h.dtype)], axis=0)

    def body(acc, xs):
        e, tok, w = xs
        xb = h_pad[tok]
        y = (jax.nn.silu(xb @ w1[l, e]) * (xb @ w3[l, e])) @ w2[l, e]
        return acc.at[tok].add(y.astype(jnp.float32) * w[:, None]), None

    acc, _ = lax.scan(body, jnp.zeros((N + 1, D), jnp.float32), (block_e, row_tok.reshape(n_blocks, mb), row_w.reshape(n_blocks, mb)))
    shared = (jax.nn.silu(h @ ws1[l]) * (h @ ws3[l])) @ ws2[l]
    return (acc[:N] + shared.astype(jnp.float32)).astype(h.dtype)


def adaln(c, w_ada, b_ada):
    m = jax.nn.silu(c) @ w_ada + b_ada
    return jnp.split(m[:, None, :], 6, axis=-1)


def decoder_layer(x, c, pos, past_k, past_v, conv_buf, ssm0, chunked, l, w_ada, b_ada, norm_mix_pre, norm_mix_post, norm_ffn_pre, norm_ffn_post, w_in, conv_w, a_log, dt_bias, gdn_norm, w_out, w_router, e_bias, w1, w3, w2, ws1, ws3, ws2):
    sh_m, sc_m, gt_m, sh_f, sc_f, gt_f = adaln(c, w_ada[l], b_ada[l])
    h = rmsnorm(x, norm_mix_pre[l]) * (1 + sc_m) + sh_m
    mix, k_new, v_new, conv_new, ssm_new = token_mixer(h, pos, past_k, past_v, conv_buf, ssm0, chunked, w_in[l], conv_w[l], a_log[l], dt_bias[l], gdn_norm[l], w_out[l])
    x = x + gt_m * rmsnorm(mix, norm_mix_post[l])
    h = rmsnorm(x, norm_ffn_pre[l]) * (1 + sc_f) + sh_f
    B, S, D = h.shape
    f = moe_ffn(h.reshape(B * S, D), l, w_router, e_bias, w1, w3, w2, ws1, ws3, ws2).reshape(B, S, D)
    x = x + gt_f * rmsnorm(f, norm_ffn_post[l])
    return x, k_new, v_new, conv_new, ssm_new


def setup_inputs(seed: int = 0) -> dict:
    key = jax.random.key(seed)
    ks = jax.random.split(key, 40)
    f32 = jnp.float32
    n_pages = PAST_LEN // PAGE_SIZE
    n_used = DEC_BATCH * n_pages
    n_pool = n_used + (n_used + 3) // 4

    def nrm(k, shape, scale):
        return jax.random.normal(k, shape, f32) * scale

    def gain(k, shape):
        return 1.0 + 0.02 * jax.random.normal(k, shape, f32)

    page_table = jax.random.permutation(ks[8], n_pool)[:n_used].reshape(DEC_BATCH, n_pages).astype(jnp.int32)
    A = jax.random.uniform(ks[15], (DEPTH, GDN_HEADS), f32, 1.0, 16.0)
    dt = jnp.exp(jax.random.uniform(ks[16], (DEPTH, GDN_HEADS), f32, math.log(1e-3), math.log(1e-1)))
    dt_bias = dt + jnp.log(-jnp.expm1(-dt))
    return {
        'x_prompt': nrm(ks[0], (BATCH, SEQ, D_MODEL), 1.0),
        'x_sample': nrm(ks[1], (DEC_BATCH, DEC_SEQ, D_MODEL), 1.0),
        'c_prompt': nrm(ks[2], (BATCH, D_MODEL), 1.0),
        'c_sample': nrm(ks[3], (DEC_BATCH, D_MODEL), 1.0),
        'cache_k': nrm(ks[4], (DEPTH, n_pool, PAGE_SIZE, KV_HEADS, HEAD_DIM), 1.0),
        'cache_v': nrm(ks[5], (DEPTH, n_pool, PAGE_SIZE, KV_HEADS, HEAD_DIM), 1.0),
        'state_ssm': nrm(ks[6], (DEPTH, DEC_BATCH, GDN_HEADS, GDN_DK, GDN_DV), 0.1),
        'state_conv': nrm(ks[7], (DEPTH, DEC_BATCH, CONV_W - 1, GDN_CONV_CH), 1.0),
        'page_table': page_table,
        'w_ada': nrm(ks[9], (DEPTH, D_MODEL, 6 * D_MODEL), 0.5 * D_MODEL ** -0.5),
        'b_ada': nrm(ks[10], (DEPTH, 6 * D_MODEL), 0.02),
        'norm_mix_pre': gain(ks[11], (DEPTH, D_MODEL)),
        'norm_mix_post': gain(ks[12], (DEPTH, D_MODEL)),
        'norm_ffn_pre': gain(ks[13], (DEPTH, D_MODEL)),
        'norm_ffn_post': gain(ks[14], (DEPTH, D_MODEL)),
        'w_in': nrm(ks[17], (DEPTH, D_MODEL, D_IN_PROJ), D_MODEL ** -0.5),
        'conv_w': nrm(ks[18], (DEPTH, CONV_W, GDN_CONV_CH), CONV_W ** -0.5),
        'a_log': jnp.log(A),
        'dt_bias': dt_bias,
        'gdn_norm': gain(ks[19], (DEPTH, GDN_DV)),
        'w_out': nrm(ks[20], (DEPTH, MIX_WIDTH, D_MODEL), MIX_WIDTH ** -0.5),
        'w_router': nrm(ks[21], (DEPTH, D_MODEL, N_EXPERTS), D_MODEL ** -0.5),
        'e_bias': nrm(ks[22], (DEPTH, N_EXPERTS), 0.01),
        'w1': nrm(ks[23], (DEPTH, N_EXPERTS, D_MODEL, D_EXPERT), D_MODEL ** -0.5),
        'w3': nrm(ks[24], (DEPTH, N_EXPERTS, D_MODEL, D_EXPERT), D_MODEL ** -0.5),
        'w2': nrm(ks[25], (DEPTH, N_EXPERTS, D_EXPERT, D_MODEL), D_EXPERT ** -0.5),
        'ws1': nrm(ks[26], (DEPTH, D_MODEL, D_SHARED), D_MODEL ** -0.5),
        'ws3': nrm(ks[27], (DEPTH, D_MODEL, D_SHARED), D_MODEL ** -0.5),
        'ws2': nrm(ks[28], (DEPTH, D_SHARED, D_MODEL), D_SHARED ** -0.5),
    }


def reference(x_prompt, x_sample, c_prompt, c_sample, cache_k, cache_v, state_ssm, state_conv, page_table, w_ada, b_ada, norm_mix_pre, norm_mix_post, norm_ffn_pre, norm_ffn_post, w_in, conv_w, a_log, dt_bias, gdn_norm, w_out, w_router, e_bias, w1, w3, w2, ws1, ws3, ws2):
    bp, sp = x_prompt.shape[0], x_prompt.shape[1]
    bs, ss = x_sample.shape[0], x_sample.shape[1]
    past_len = page_table.shape[1] * PAGE_SIZE
    pos_p = jnp.arange(sp, dtype=jnp.int32)
    pos_s = past_len + jnp.arange(ss, dtype=jnp.int32)
    weights = (w_ada, b_ada, norm_mix_pre, norm_mix_post, norm_ffn_pre, norm_ffn_post, w_in, conv_w, a_log, dt_bias, gdn_norm, w_out, w_router, e_bias, w1, w3, w2, ws1, ws3, ws2)
    yp, ys = x_prompt, x_sample
    kp_l, vp_l, ks_l, vs_l, sp_l, ss_l, cp_l, cs_l = [], [], [], [], [], [], [], []
    for l in range(DEPTH):
        conv0 = jnp.zeros((bp, CONV_W - 1, GDN_CONV_CH), x_prompt.dtype)
        ssm0 = jnp.zeros((bp, GDN_HEADS, GDN_DK, GDN_DV), jnp.float32)
        yp, kp, vp, cp, s_p = decoder_layer(yp, c_prompt, pos_p, None, None, conv0, ssm0, True, l, *weights)
        past_k = cache_k[l, page_table].reshape(bs, past_len, KV_HEADS, HEAD_DIM)
        past_v = cache_v[l, page_table].reshape(bs, past_len, KV_HEADS, HEAD_DIM)
        ys, k_s, v_s, c_s, s_s = decoder_layer(ys, c_sample, pos_s, past_k, past_v, state_conv[l], state_ssm[l].astype(jnp.float32), False, l, *weights)
        kp_l.append(kp)
        vp_l.append(vp)
        ks_l.append(k_s)
        vs_l.append(v_s)
        sp_l.append(s_p.astype(state_ssm.dtype))
        ss_l.append(s_s.astype(state_ssm.dtype))
        cp_l.append(cp.astype(state_conv.dtype))
        cs_l.append(c_s.astype(state_conv.dtype))
    k_prompt = jnp.stack(kp_l)
    v_prompt = jnp.stack(vp_l)
    k_sample = jnp.stack(ks_l)
    v_sample = jnp.stack(vs_l)
    ssm_prompt = jnp.stack(sp_l)
    ssm_sample = jnp.stack(ss_l)
    conv_prompt = jnp.stack(cp_l)
    conv_sample = jnp.stack(cs_l)
    return (yp, ys, k_prompt, v_prompt, k_sample, v_sample, ssm_prompt, ssm_sample, conv_prompt, conv_sample)
```

```python
import functools

import jax
import jax.numpy as jnp
from jax import lax
from jax.experimental import pallas as pl
from jax.experimental.pallas import tpu as pltpu

F32 = jnp.float32
BF16 = jnp.bfloat16
HIGHEST = lax.Precision.HIGHEST

HEAD_DIM = 128
ATTN_HEADS = 16
KV_HEADS = 4
GDN_HEADS = 16
CONV_W = 4
GDN_CHUNK = 64
MOBA_BLOCK = 256
MOBA_TOPK = 3
ROPE_THETA = 500000.0
ROT_DIM = HEAD_DIM // 4
PAGE_SIZE = 128
N_GROUPS = 8
TOPK_GROUPS = 4
TOP_K = 8
ROUTED_SCALE = 2.5
NORM_EPS = 1e-6

Q_ATTN = ATTN_HEADS * HEAD_DIM
KV_ATTN = KV_HEADS * HEAD_DIM
QK_GDN = GDN_HEADS * HEAD_DIM
V_GDN = GDN_HEADS * HEAD_DIM
GDN_CONV_CH = 2 * QK_GDN + V_GDN
OFF_K = Q_ATTN
OFF_V = Q_ATTN + KV_ATTN
OFF_CONV = Q_ATTN + 2 * KV_ATTN
OFF_Z = OFF_CONV + GDN_CONV_CH
OFF_AB = OFF_Z + V_GDN

ROW_BLK = 128
MOE_BLK = 256
NEG = -1e30
VMEM_LIMIT = 56 << 20


def _params(*sem):
    return pltpu.CompilerParams(dimension_semantics=sem, vmem_limit_bytes=VMEM_LIMIT)


def _pick(n, prefs):
    for p in prefs:
        if n % p == 0:
            return p
    return n


def _nt(a, b, **kw):
    return lax.dot_general(a, b, (((1,), (1,)), ((), ())), preferred_element_type=F32, **kw)


def _silu(x):
    return x * jax.nn.sigmoid(x)


def _rms(x, g):
    return x * lax.rsqrt(jnp.mean(x * x, axis=-1, keepdims=True) + NORM_EPS) * g


def _beats(a, b, a_first):
    return jnp.where(a > b, 1.0, jnp.where(a == b, jnp.where(a_first, 1.0, 0.0), 0.0))


def _mm_body(x_ref, w_ref, b_ref, o_ref, wbf_ref, *, silu_in):
    @pl.when(pl.program_id(1) == 0)
    def _cast_weight_tile():
        wbf_ref[...] = w_ref[...].astype(BF16)

    x = x_ref[...]
    if silu_in:
        x = _silu(x.astype(F32))
    acc = jnp.dot(x.astype(BF16), wbf_ref[...], preferred_element_type=F32)
    o_ref[...] = (acc + b_ref[...]).astype(o_ref.dtype)


def _matmul(x, w, b=None, *, silu_in=False, out_dtype=F32, tm=None, tn=None):
    M, K = x.shape
    N = w.shape[1]
    tm = tm or _pick(M, (640, 512, 256, 128))
    tn = tn or _pick(N, (512, 256, 128))
    if b is None:
        b = jnp.zeros((1, N), F32)
    return pl.pallas_call(
        functools.partial(_mm_body, silu_in=silu_in),
        grid=(N // tn, M // tm),
        in_specs=[
            pl.BlockSpec((tm, K), lambda n, m: (m, 0)),
            pl.BlockSpec((K, tn), lambda n, m: (0, n)),
            pl.BlockSpec((1, tn), lambda n, m: (0, n)),
        ],
        out_specs=pl.BlockSpec((tm, tn), lambda n, m: (m, n)),
        out_shape=jax.ShapeDtypeStruct((M, N), out_dtype),
        scratch_shapes=[pltpu.VMEM((K, tn), BF16)],
        compiler_params=_params("arbitrary", "arbitrary"),
    )(x, w, b)


def _prenorm_body(x_ref, g_ref, sh_ref, sc_ref, o_ref):
    y = _rms(x_ref[...], g_ref[...]) * (1.0 + sc_ref[...]) + sh_ref[...]
    o_ref[...] = y.astype(o_ref.dtype)


def _mod_spec(D, slot, rb_per_seq):
    return pl.BlockSpec((ROW_BLK, D), lambda r: (r // rb_per_seq, slot))


def _row_spec(D):
    return pl.BlockSpec((ROW_BLK, D), lambda r: (r, 0))


def _vec_spec(D):
    return pl.BlockSpec((1, D), lambda r: (0, 0))


def _prenorm(x, g, mod, slot_sh, slot_sc, rb_per_seq):
    T, D = x.shape
    return pl.pallas_call(
        _prenorm_body,
        grid=(T // ROW_BLK,),
        in_specs=[_row_spec(D), _vec_spec(D), _mod_spec(D, slot_sh, rb_per_seq),
                  _mod_spec(D, slot_sc, rb_per_seq)],
        out_specs=_row_spec(D),
        out_shape=jax.ShapeDtypeStruct((T, D), BF16),
        compiler_params=_params("arbitrary"),
    )(x, g, mod, mod)


def _mid_body(x_ref, mix_ref, gpost_ref, gt_ref, gpre_ref, sh_ref, sc_ref, wr_ref,
              x1_ref, h_ref, lg_ref):
    x1 = x_ref[...] + gt_ref[...] * _rms(mix_ref[...], gpost_ref[...])
    x1_ref[...] = x1
    h = _rms(x1, gpre_ref[...]) * (1.0 + sc_ref[...]) + sh_ref[...]
    h_ref[...] = h.astype(BF16)
    lg_ref[...] = jnp.dot(h, wr_ref[...], precision=HIGHEST, preferred_element_type=F32)


def _mid(x, mix, g_post, g_pre, mod, w_router, rb_per_seq):
    T, D = x.shape
    E = w_router.shape[1]
    return pl.pallas_call(
        _mid_body,
        grid=(T // ROW_BLK,),
        in_specs=[_row_spec(D), _row_spec(D), _vec_spec(D), _mod_spec(D, 2, rb_per_seq),
                  _vec_spec(D), _mod_spec(D, 3, rb_per_seq), _mod_spec(D, 4, rb_per_seq),
                  pl.BlockSpec((D, E), lambda r: (0, 0))],
        out_specs=[_row_spec(D), _row_spec(D), _row_spec(E)],
        out_shape=[jax.ShapeDtypeStruct((T, D), F32), jax.ShapeDtypeStruct((T, D), BF16),
                   jax.ShapeDtypeStruct((T, E), F32)],
        compiler_params=_params("arbitrary"),
    )(x, mix, g_post, mod, g_pre, mod, mod, w_router)


def _final_body(x1_ref, a_ref, b_ref, g_ref, gt_ref, o_ref):
    f = a_ref[...] + b_ref[...]
    o_ref[...] = x1_ref[...] + gt_ref[...] * _rms(f, g_ref[...])


def _final(x1, routed, shared, g_post, mod, rb_per_seq):
    T, D = x1.shape
    return pl.pallas_call(
        _final_body,
        grid=(T // ROW_BLK,),
        in_specs=[_row_spec(D), _row_spec(D), _row_spec(D), _vec_spec(D),
                  _mod_spec(D, 5, rb_per_seq)],
        out_specs=_row_spec(D),
        out_shape=jax.ShapeDtypeStruct((T, D), F32),
        compiler_params=_params("arbitrary"),
    )(x1, routed, shared, g_post, mod)


def _rope_body(x_ref, c_ref, sa_ref, sb_ref, o_ref):
    x = x_ref[...]
    half = ROT_DIM // 2
    o_ref[...] = (x * c_ref[...] + pltpu.roll(x, HEAD_DIM - half, 1) * sa_ref[...]
                  + pltpu.roll(x, half, 1) * sb_ref[...])


def _rope(proj, cos_t, sin_a, sin_b):
    T = proj.shape[0]
    n_heads = ATTN_HEADS + KV_HEADS
    tr = _pick(T, (640, 512, 256, 128))
    tab = pl.BlockSpec((tr, HEAD_DIM), lambda r, h: (r, 0))
    blk = pl.BlockSpec((tr, HEAD_DIM), lambda r, h: (r, h))
    return pl.pallas_call(
        _rope_body,
        grid=(T // tr, n_heads),
        in_specs=[blk, tab, tab, tab],
        out_specs=blk,
        out_shape=jax.ShapeDtypeStruct((T, n_heads * HEAD_DIM), F32),
        compiler_params=_params("arbitrary", "arbitrary"),
    )(proj, cos_t, sin_a, sin_b)


def _topk_mask(gate, lane, n_valid, n_cand):
    gate = jnp.where(lane < n_valid, gate, -jnp.inf)
    rank = jnp.zeros(gate.shape, F32)
    for m in range(n_cand):
        rank = rank + _beats(gate[:, m:m + 1], gate, m < lane)
    return jnp.where(lane < n_valid, jnp.where(rank < MOBA_TOPK, 1.0, 0.0), 0.0)


def _attn_prefill_body(q_ref, k_ref, v_ref, o_ref, m_ref, l_ref, acc_ref, sel_ref, *, nblk, grp):
    i = pl.program_id(2)
    blk = MOBA_BLOCK
    rows = grp * blk
    scale = HEAD_DIM ** -0.5
    qf = jnp.concatenate([q_ref[:, h * HEAD_DIM:(h + 1) * HEAD_DIM] for h in range(grp)], axis=0)
    qb = qf.astype(BF16)

    krow = lax.broadcasted_iota(jnp.int32, (HEAD_DIM, HEAD_DIM), 0)
    kmean = jnp.zeros((HEAD_DIM, HEAD_DIM), F32)
    for n in range(nblk):
        kmean = jnp.where(krow == n, jnp.mean(k_ref[n * blk:(n + 1) * blk, :], axis=0, keepdims=True), kmean)
    gate = _nt(qf, kmean, precision=HIGHEST)
    lane = lax.broadcasted_iota(jnp.int32, (rows, HEAD_DIM), 1)
    sel_ref[...] = _topk_mask(gate, lane, i, nblk)

    start = pl.multiple_of(i * blk, blk)
    s = _nt(qb, k_ref[pl.ds(start, blk), :].astype(BF16)) * scale
    qpos = jnp.bitwise_and(lax.broadcasted_iota(jnp.int32, (rows, blk), 0), blk - 1)
    kpos = lax.broadcasted_iota(jnp.int32, (rows, blk), 1)
    s = jnp.where(kpos <= qpos, s, NEG)
    m0 = jnp.max(s, axis=1, keepdims=True)
    p = jnp.exp(s - m0)
    m_ref[...] = m0
    l_ref[...] = jnp.sum(p, axis=1, keepdims=True)
    acc_ref[...] = jnp.dot(p.astype(BF16), v_ref[pl.ds(start, blk), :].astype(BF16),
                           preferred_element_type=F32)

    for n in range(nblk - 1):
        @pl.when(n < i)
        def _past_block(n=n):
            s = _nt(qb, k_ref[n * blk:(n + 1) * blk, :].astype(BF16)) * scale
            s = jnp.where(sel_ref[:, n:n + 1] > 0.0, s, NEG)
            m_old = m_ref[...]
            m_new = jnp.maximum(m_old, jnp.max(s, axis=1, keepdims=True))
            alpha = jnp.exp(m_old - m_new)
            p = jnp.exp(s - m_new)
            l_ref[...] = alpha * l_ref[...] + jnp.sum(p, axis=1, keepdims=True)
            acc_ref[...] = alpha * acc_ref[...] + jnp.dot(
                p.astype(BF16), v_ref[n * blk:(n + 1) * blk, :].astype(BF16),
                preferred_element_type=F32)
            m_ref[...] = m_new

    out = acc_ref[...] / l_ref[...]
    for h in range(grp):
        o_ref[:, h * HEAD_DIM:(h + 1) * HEAD_DIM] = out[h * blk:(h + 1) * blk].astype(o_ref.dtype)


def _attn_prefill(qk_rot, proj, B, S):
    nblk = S // MOBA_BLOCK
    grp = ATTN_HEADS // KV_HEADS
    rows = grp * MOBA_BLOCK
    k_col0 = OFF_K // HEAD_DIM
    v_col0 = OFF_V // HEAD_DIM
    return pl.pallas_call(
        functools.partial(_attn_prefill_body, nblk=nblk, grp=grp),
        grid=(B, KV_HEADS, nblk),
        in_specs=[
            pl.BlockSpec((MOBA_BLOCK, grp * HEAD_DIM), lambda b, g, i: (b * nblk + i, g)),
            pl.BlockSpec((S, HEAD_DIM), lambda b, g, i: (b, k_col0 + g)),
            pl.BlockSpec((S, HEAD_DIM), lambda b, g, i: (b, v_col0 + g)),
        ],
        out_specs=pl.BlockSpec((MOBA_BLOCK, grp * HEAD_DIM), lambda b, g, i: (b * nblk + i, g)),
        out_shape=jax.ShapeDtypeStruct((B * S, Q_ATTN), BF16),
        scratch_shapes=[pltpu.VMEM((rows, 1), F32), pltpu.VMEM((rows, 1), F32),
                        pltpu.VMEM((rows, HEAD_DIM), F32), pltpu.VMEM((rows, HEAD_DIM), F32)],
        compiler_params=_params("arbitrary", "arbitrary", "arbitrary"),
    )(qk_rot, qk_rot, proj)


def _attn_decode_body(pt_ref, q_ref, kx_ref, vx_ref, k0_ref, k1_ref, v0_ref, v1_ref, o_ref,
                      m_sc, l_sc, g_sc, o_sc, *, nblk):
    del pt_ref
    n = pl.program_id(1)
    H = ATTN_HEADS
    grp = ATTN_HEADS // KV_HEADS
    scale = HEAD_DIM ** -0.5
    q = q_ref[...]
    qb = q.astype(BF16)
    row = lax.broadcasted_iota(jnp.int32, (H, 1), 0)
    lane = lax.broadcasted_iota(jnp.int32, (H, HEAD_DIM), 1)

    @pl.when(n == 0)
    def _init():
        m_sc[...] = jnp.zeros_like(m_sc)
        l_sc[...] = jnp.zeros_like(l_sc)
        g_sc[...] = jnp.zeros_like(g_sc)

    m_n = jnp.zeros((H, 1), F32)
    l_n = jnp.zeros((H, 1), F32)
    g_n = jnp.zeros((H, 1), F32)
    o_n = jnp.zeros((H, HEAD_DIM), F32)
    for h in range(KV_HEADS):
        cols = slice(h * HEAD_DIM, (h + 1) * HEAD_DIM)
        kf = jnp.concatenate([k0_ref[:, cols], k1_ref[:, cols]], axis=0)
        vf = jnp.concatenate([v0_ref[:, cols], v1_ref[:, cols]], axis=0)
        gate = jnp.sum(q * jnp.mean(kf, axis=0, keepdims=True), axis=1, keepdims=True)
        s = _nt(qb, kf.astype(BF16)) * scale
        mh = jnp.max(s, axis=1, keepdims=True)
        p = jnp.exp(s - mh)
        oh = jnp.dot(p.astype(BF16), vf.astype(BF16), preferred_element_type=F32)
        mine = jnp.logical_and(row >= h * grp, row < (h + 1) * grp)
        m_n = jnp.where(mine, mh, m_n)
        l_n = jnp.where(mine, jnp.sum(p, axis=1, keepdims=True), l_n)
        g_n = jnp.where(mine, gate, g_n)
        o_n = jnp.where(mine, oh, o_n)
    here = lane == n
    m_sc[...] = jnp.where(here, m_n, m_sc[...])
    l_sc[...] = jnp.where(here, l_n, l_sc[...])
    g_sc[...] = jnp.where(here, g_n, g_sc[...])
    o_sc[n] = o_n

    @pl.when(n == nblk - 1)
    def _combine():
        sel = _topk_mask(g_sc[...], lane, nblk, nblk)
        m_all = m_sc[...]
        s_own = jnp.sum(q * kx_ref[...], axis=1, keepdims=True) * scale
        m_tot = jnp.maximum(jnp.max(jnp.where(sel > 0.0, m_all, NEG), axis=1, keepdims=True), s_own)
        a = jnp.where(sel > 0.0, jnp.exp(jnp.minimum(m_all - m_tot, 0.0)), 0.0)
        e_own = jnp.exp(s_own - m_tot)
        denom = jnp.sum(a * l_sc[...], axis=1, keepdims=True) + e_own
        out = e_own * vx_ref[...]
        for j in range(nblk):
            out = out + a[:, j:j + 1] * o_sc[j]
        o_ref[...] = (out / denom).astype(o_ref.dtype)


def _attn_decode(q_s, kx, vx, cache_k, cache_v, page_table):
    Bs, n_pages = page_table.shape
    pages_per_blk = MOBA_BLOCK // PAGE_SIZE
    assert pages_per_blk == 2
    nblk = n_pages // pages_per_blk
    vec = pl.BlockSpec((None, ATTN_HEADS, HEAD_DIM), lambda b, n, pt: (b, 0, 0))

    def page(j):
        return pl.BlockSpec((None, PAGE_SIZE, KV_ATTN), lambda b, n, pt: (pt[b, 2 * n + j], 0, 0))

    return pl.pallas_call(
        functools.partial(_attn_decode_body, nblk=nblk),
        grid_spec=pltpu.PrefetchScalarGridSpec(
            num_scalar_prefetch=1,
            grid=(Bs, nblk),
            in_specs=[vec, vec, vec, page(0), page(1), page(0), page(1)],
            out_specs=vec,
            scratch_shapes=[pltpu.VMEM((ATTN_HEADS, HEAD_DIM), F32)] * 3
            + [pltpu.VMEM((nblk, ATTN_HEADS, HEAD_DIM), F32)],
        ),
        out_shape=jax.ShapeDtypeStruct((Bs, ATTN_HEADS, HEAD_DIM), BF16),
        compiler_params=_params("arbitrary", "arbitrary"),
    )(page_table, q_s, kx, vx, cache_k, cache_k, cache_v, cache_v)


def _conv_finish(y, c):
    y = _silu(y)
    nrm = y * lax.rsqrt(jnp.sum(y * y, axis=1, keepdims=True) + 1e-6)
    return jnp.where(c < GDN_HEADS, nrm * (HEAD_DIM ** -0.5), jnp.where(c < 2 * GDN_HEADS, nrm, y))


def _conv_prompt_body(x_ref, w_ref, o_ref):
    x = x_ref[...]
    row = lax.broadcasted_iota(jnp.int32, x.shape, 0)
    y = x * w_ref[CONV_W - 1:CONV_W, :]
    for k in range(1, CONV_W):
        y = y + jnp.where(row >= k, pltpu.roll(x, k, 0), 0.0) * w_ref[CONV_W - 1 - k:CONV_W - k, :]
    o_ref[...] = _conv_finish(y, pl.program_id(1))


def _conv_prompt(proj, conv_w, B, S):
    n_tiles = GDN_CONV_CH // HEAD_DIM
    col0 = OFF_CONV // HEAD_DIM
    return pl.pallas_call(
        _conv_prompt_body,
        grid=(B, n_tiles),
        in_specs=[pl.BlockSpec((S, HEAD_DIM), lambda b, c: (b, col0 + c)),
                  pl.BlockSpec((CONV_W, HEAD_DIM), lambda b, c: (0, c))],
        out_specs=pl.BlockSpec((S, HEAD_DIM), lambda b, c: (b, c)),
        out_shape=jax.ShapeDtypeStruct((B * S, GDN_CONV_CH), F32),
        compiler_params=_params("arbitrary", "arbitrary"),
    )(proj, conv_w)


def _conv_sample_body(x_ref, buf_ref, w_ref, o_ref):
    y = x_ref[...] * w_ref[CONV_W - 1:CONV_W, :]
    for i in range(CONV_W - 1):
        y = y + buf_ref[i] * w_ref[i:i + 1, :]
    o_ref[...] = _conv_finish(y, pl.program_id(0))


def _conv_sample(proj, buf_t, conv_w, row_blk0):
    Bs = buf_t.shape[1]
    n_tiles = GDN_CONV_CH // HEAD_DIM
    col0 = OFF_CONV // HEAD_DIM
    return pl.pallas_call(
        _conv_sample_body,
        grid=(n_tiles,),
        in_specs=[pl.BlockSpec((Bs, HEAD_DIM), lambda c: (row_blk0, col0 + c)),
                  pl.BlockSpec((CONV_W - 1, Bs, HEAD_DIM), lambda c: (0, 0, c)),
                  pl.BlockSpec((CONV_W, HEAD_DIM), lambda c: (0, c))],
        out_specs=pl.BlockSpec((Bs, HEAD_DIM), lambda c: (0, c)),
        out_shape=jax.ShapeDtypeStruct((Bs, GDN_CONV_CH), F32),
        compiler_params=_params("arbitrary"),
    )(proj, buf_t, conv_w)


def _gates_body(ab_ref, alog_ref, dtb_ref, o_ref):
    x = ab_ref[...]
    sp_in = x + dtb_ref[...]
    softplus = jnp.maximum(sp_in, 0.0) + jnp.log1p(jnp.exp(-jnp.abs(sp_in)))
    g = -jnp.exp(alog_ref[...]) * softplus
    lane = lax.broadcasted_iota(jnp.int32, x.shape, 1)
    o_ref[...] = jnp.where(lane < GDN_HEADS, g, jax.nn.sigmoid(x))


def _gates(ab, alog_pad, dtb_pad):
    T = ab.shape[0]
    return pl.pallas_call(
        _gates_body,
        grid=(T // ROW_BLK,),
        in_specs=[_row_spec(HEAD_DIM), _vec_spec(HEAD_DIM), _vec_spec(HEAD_DIM)],
        out_specs=_row_spec(HEAD_DIM),
        out_shape=jax.ShapeDtypeStruct((T, HEAD_DIM), F32),
        compiler_params=_params("arbitrary"),
    )(ab, alog_pad, dtb_pad)


def _mm32(a, b):
    return jnp.dot(a, b, precision=HIGHEST, preferred_element_type=F32)


def _unit_lower_inverse(low):
    C = low.shape[0]
    ii = lax.broadcasted_iota(jnp.int32, (C, C), 0)
    jj = lax.broadcasted_iota(jnp.int32, (C, C), 1)
    inv = jnp.where(ii == jj, 1.0, 0.0) - low
    pw = _mm32(low, low)
    span = 2
    while span < C:
        inv = inv + _mm32(inv, pw)
        span *= 2
        if span < C:
            pw = _mm32(pw, pw)
    return inv


def _gdn_chunk_body(q_ref, k_ref, v_ref, g_ref, b_ref, o_ref, s_out_ref, s_ref):
    n = pl.program_id(1)
    C = GDN_CHUNK

    @pl.when(n == 0)
    def _zero_state():
        s_ref[...] = jnp.zeros_like(s_ref)

    ii = lax.broadcasted_iota(jnp.int32, (C, C), 0)
    jj = lax.broadcasted_iota(jnp.int32, (C, C), 1)
    eye = ii == jj
    tril = jj <= ii
    for h in range(GDN_HEADS):
        cols = slice(h * HEAD_DIM, (h + 1) * HEAD_DIM)
        q = q_ref[:, cols]
        k = k_ref[:, cols]
        v = v_ref[:, cols]
        g_row = g_ref[h:h + 1, :]
        gc_col = jnp.sum(jnp.where(tril, g_row, 0.0), axis=1, keepdims=True)
        gc_row = jnp.sum(jnp.where(eye, gc_col, 0.0), axis=0, keepdims=True)
        beta = jnp.sum(jnp.where(eye, b_ref[h:h + 1, :], 0.0), axis=1, keepdims=True)
        decay = jnp.exp(jnp.where(tril, gc_col - gc_row, -jnp.inf))
        g_last = gc_col[C - 1:C, :]
        e_gc = jnp.exp(gc_col)
        kbeta = k * beta
        low = jnp.where(ii > jj, _nt(kbeta, k, precision=HIGHEST) * decay, 0.0)
        tmat = _unit_lower_inverse(low)
        u = _mm32(tmat, v * beta)
        w = _mm32(tmat, kbeta * e_gc)
        attn = _nt(q, k, precision=HIGHEST) * decay
        state = s_ref[h]
        v_new = u - _mm32(w, state)
        o_ref[:, cols] = _mm32(q * e_gc, state) + _mm32(attn, v_new)
        k_tail = k * jnp.exp(g_last - gc_col)
        s_ref[h] = state * jnp.exp(g_last) + _mm32(k_tail.T, v_new)

    @pl.when(n == pl.num_programs(1) - 1)
    def _emit_state():
        s_out_ref[...] = s_ref[...]


def _gdn_prompt(qkv, g_rows, b_rows, B, S):
    N = S // GDN_CHUNK
    W = GDN_HEADS * HEAD_DIM

    def part(j):
        return pl.BlockSpec((GDN_CHUNK, W), lambda b, n: (b * N + n, j))

    gspec = pl.BlockSpec((None, None, GDN_HEADS, GDN_CHUNK), lambda b, n: (b, n, 0, 0))
    return pl.pallas_call(
        _gdn_chunk_body,
        grid=(B, N),
        in_specs=[part(0), part(1), part(2), gspec, gspec],
        out_specs=[pl.BlockSpec((GDN_CHUNK, W), lambda b, n: (b * N + n, 0)),
                   pl.BlockSpec((None, GDN_HEADS, HEAD_DIM, HEAD_DIM), lambda b, n: (b, 0, 0, 0))],
        out_shape=[jax.ShapeDtypeStruct((B * S, W), F32),
                   jax.ShapeDtypeStruct((B, GDN_HEADS, HEAD_DIM, HEAD_DIM), F32)],
        scratch_shapes=[pltpu.VMEM((GDN_HEADS, HEAD_DIM, HEAD_DIM), F32)],
        compiler_params=_params("arbitrary", "arbitrary"),
    )(qkv, qkv, qkv, g_rows, b_rows)


def _gdn_step_body(qkv_ref, gb_ref, s_ref, o_ref, s_out_ref):
    H = GDN_HEADS
    ii = lax.broadcasted_iota(jnp.int32, (HEAD_DIM, HEAD_DIM), 0)
    jj = lax.broadcasted_iota(jnp.int32, (HEAD_DIM, HEAD_DIM), 1)
    eye = ii == jj
    lane = lax.broadcasted_iota(jnp.int32, (1, HEAD_DIM), 1)
    gb = gb_ref[...]
    for h in range(H):
        q_col = jnp.sum(jnp.where(eye, qkv_ref[h:h + 1, :], 0.0), axis=1, keepdims=True)
        k_col = jnp.sum(jnp.where(eye, qkv_ref[H + h:H + h + 1, :], 0.0), axis=1, keepdims=True)
        v_row = qkv_ref[2 * H + h:2 * H + h + 1, :]
        g = jnp.sum(jnp.where(lane == h, gb, 0.0), axis=1, keepdims=True)
        beta = jnp.sum(jnp.where(lane == H + h, gb, 0.0), axis=1, keepdims=True)
        state = s_ref[h] * jnp.exp(g)
        kv = jnp.sum(k_col * state, axis=0, keepdims=True)
        delta = (v_row - kv) * beta
        state = state + k_col * delta
        s_out_ref[h] = state
        o_ref[h:h + 1, :] = jnp.sum(q_col * state, axis=0, keepdims=True)


def _gdn_sample(qkv3, gb3, state):
    Bs = qkv3.shape[0]
    H = GDN_HEADS
    sspec = pl.BlockSpec((None, H, HEAD_DIM, HEAD_DIM), lambda b: (b, 0, 0, 0))
    return pl.pallas_call(
        _gdn_step_body,
        grid=(Bs,),
        in_specs=[pl.BlockSpec((None, 3 * H, HEAD_DIM), lambda b: (b, 0, 0)),
                  pl.BlockSpec((None, 1, HEAD_DIM), lambda b: (b, 0, 0)), sspec],
        out_specs=[pl.BlockSpec((None, H, HEAD_DIM), lambda b: (b, 0, 0)), sspec],
        out_shape=[jax.ShapeDtypeStruct((Bs, H, HEAD_DIM), F32),
                   jax.ShapeDtypeStruct(state.shape, F32)],
        compiler_params=_params("arbitrary"),
    )(qkv3, gb3, state)


def _gated_norm_body(o_ref, z_ref, g_ref, out_ref):
    z = z_ref[...]
    out_ref[...] = (_rms(o_ref[...], g_ref[...]) * _silu(z)).astype(out_ref.dtype)


def _gated_norm(o_g, proj, gdn_norm):
    T = o_g.shape[0]
    z_col0 = OFF_Z // HEAD_DIM
    return pl.pallas_call(
        _gated_norm_body,
        grid=(T // ROW_BLK, GDN_HEADS),
        in_specs=[pl.BlockSpec((ROW_BLK, HEAD_DIM), lambda r, h: (r, h)),
                  pl.BlockSpec((ROW_BLK, HEAD_DIM), lambda r, h: (r, z_col0 + h)),
                  pl.BlockSpec((1, HEAD_DIM), lambda r, h: (0, 0))],
        out_specs=pl.BlockSpec((ROW_BLK, HEAD_DIM), lambda r, h: (r, h)),
        out_shape=jax.ShapeDtypeStruct((T, V_GDN), BF16),
        compiler_params=_params("arbitrary", "arbitrary"),
    )(o_g, proj, gdn_norm)


def _route_body(lg_ref, bias_ref, idx_ref, wt_ref):
    scores = jax.nn.sigmoid(lg_ref[...])
    biased = scores + bias_ref[...]
    shape = scores.shape
    E = shape[1]
    per_grp = E // N_GROUPS
    lane_i = lax.broadcasted_iota(jnp.int32, shape, 1)
    lane = lane_i.astype(F32)
    ninf = -jnp.inf

    def in_grp(g):
        return jnp.logical_and(lane_i >= g * per_grp, lane_i < (g + 1) * per_grp)

    gscore = []
    for g in range(N_GROUPS):
        xg = jnp.where(in_grp(g), biased, ninf)
        m1 = jnp.max(xg, axis=1, keepdims=True)
        i1 = jnp.min(jnp.where(xg == m1, lane, float(E)), axis=1, keepdims=True)
        m2 = jnp.max(jnp.where(lane == i1, ninf, xg), axis=1, keepdims=True)
        gscore.append(m1 + m2)
    keep = jnp.zeros(shape, F32)
    for g in range(N_GROUPS):
        rank = jnp.zeros_like(gscore[g])
        for o in range(N_GROUPS):
            if o != g:
                rank = rank + _beats(gscore[o], gscore[g], o < g)
        keep = jnp.where(in_grp(g), jnp.where(rank < TOPK_GROUPS, 1.0, 0.0), keep)

    cur = jnp.where(keep > 0.0, biased, ninf)
    idx = jnp.zeros(shape, F32)
    wts = jnp.zeros(shape, F32)
    wsum = jnp.zeros((shape[0], 1), F32)
    for k in range(TOP_K):
        mk = jnp.max(cur, axis=1, keepdims=True)
        ik = jnp.min(jnp.where(cur == mk, lane, float(E)), axis=1, keepdims=True)
        hit = lane == ik
        wk = jnp.sum(jnp.where(hit, scores, 0.0), axis=1, keepdims=True)
        cur = jnp.where(hit, ninf, cur)
        idx = jnp.where(lane_i == k, ik, idx)
        wts = jnp.where(lane_i == k, wk, wts)
        wsum = wsum + wk
    idx_ref[...] = idx.astype(jnp.int32)
    wt_ref[...] = wts / wsum * ROUTED_SCALE


def _route(logits, e_bias):
    T, E = logits.shape
    return pl.pallas_call(
        _route_body,
        grid=(T // ROW_BLK,),
        in_specs=[_row_spec(E), _vec_spec(E)],
        out_specs=[_row_spec(E), _row_spec(E)],
        out_shape=[jax.ShapeDtypeStruct((T, E), jnp.int32), jax.ShapeDtypeStruct((T, E), F32)],
        compiler_params=_params("arbitrary"),
    )(logits, e_bias)


def _ffn_body(x_ref, w1_ref, w3_ref, w2_ref, o_ref):
    x = x_ref[...]
    a = _silu(jnp.dot(x, w1_ref[...], preferred_element_type=F32))
    a = a * jnp.dot(x, w3_ref[...], preferred_element_type=F32)
    o_ref[...] = jnp.dot(a.astype(BF16), w2_ref[...], preferred_element_type=F32)


def _shared_ffn(x, w1, w3, w2):
    T, D = x.shape
    F = w1.shape[1]
    tm = _pick(T, (640, 512, 256, 128))
    return pl.pallas_call(
        _ffn_body,
        grid=(T // tm,),
        in_specs=[pl.BlockSpec((tm, D), lambda r: (r, 0)),
                  pl.BlockSpec((D, F), lambda r: (0, 0)), pl.BlockSpec((D, F), lambda r: (0, 0)),
                  pl.BlockSpec((F, D), lambda r: (0, 0))],
        out_specs=pl.BlockSpec((tm, D), lambda r: (r, 0)),
        out_shape=jax.ShapeDtypeStruct((T, D), F32),
        compiler_params=_params("arbitrary"),
    )(x, w1, w3, w2)


def _grouped_ffn_body(be_ref, nu_ref, x_ref, rw_ref, w1_ref, w3_ref, w2_ref, o_ref):
    del be_ref

    @pl.when(pl.program_id(0) < nu_ref[0])
    def _used_block():
        x = x_ref[...]
        a = _silu(jnp.dot(x, w1_ref[...], preferred_element_type=F32))
        a = a * jnp.dot(x, w3_ref[...], preferred_element_type=F32)
        y = jnp.dot(a.astype(BF16), w2_ref[...], preferred_element_type=F32)
        o_ref[...] = y * rw_ref[...]

    @pl.when(pl.program_id(0) >= nu_ref[0])
    def _unused_block():
        o_ref[...] = jnp.zeros_like(o_ref)


def _grouped_ffn(x_sorted, row_w, block_e, n_used, w1, w3, w2):
    R, D = x_sorted.shape
    F = w1.shape[2]
    nb = R // MOE_BLK
    return pl.pallas_call(
        _grouped_ffn_body,
        grid_spec=pltpu.PrefetchScalarGridSpec(
            num_scalar_prefetch=2,
            grid=(nb,),
            in_specs=[pl.BlockSpec((MOE_BLK, D), lambda i, be, nu: (i, 0)),
                      pl.BlockSpec((MOE_BLK, 1), lambda i, be, nu: (i, 0)),
                      pl.BlockSpec((None, D, F), lambda i, be, nu: (be[i], 0, 0)),
                      pl.BlockSpec((None, D, F), lambda i, be, nu: (be[i], 0, 0)),
                      pl.BlockSpec((None, F, D), lambda i, be, nu: (be[i], 0, 0))],
            out_specs=pl.BlockSpec((MOE_BLK, D), lambda i, be, nu: (i, 0)),
        ),
        out_shape=jax.ShapeDtypeStruct((R, D), F32),
        compiler_params=_params("arbitrary"),
    )(block_e, n_used, x_sorted, row_w, w1, w3, w2)


def _moe_routed(h, idx, wts, w1, w3, w2):
    T, D = h.shape
    E = w1.shape[0]
    A = T * TOP_K
    nb = -(-A // MOE_BLK) + E
    rows = nb * MOE_BLK
    flat_e = idx.reshape(-1)
    order = jnp.argsort(flat_e)
    st = (order // TOP_K).astype(jnp.int32)
    sw = wts.reshape(-1)[order]
    onehot = (idx[:, :, None] == jnp.arange(E, dtype=jnp.int32)[None, None, :]).any(axis=1)
    csum = jnp.cumsum(onehot.astype(jnp.int32), axis=0)
    counts = csum[-1]
    pcounts = (counts + MOE_BLK - 1) // MOE_BLK * MOE_BLK
    start = jnp.cumsum(counts) - counts
    pend = jnp.cumsum(pcounts)
    pstart = pend - pcounts
    block_e = jnp.minimum(jnp.searchsorted(pend, jnp.arange(nb, dtype=jnp.int32) * MOE_BLK, side='right'),
                          E - 1).astype(jnp.int32)
    n_used = (pend[-1:] // MOE_BLK).astype(jnp.int32)
    r = jnp.arange(rows, dtype=jnp.int32)
    e_r = block_e[r // MOE_BLK]
    j = r - pstart[e_r]
    valid = j < counts[e_r]
    src = jnp.clip(start[e_r] + j, 0, A - 1)
    row_tok = jnp.where(valid, st[src], 0)
    row_w = jnp.where(valid, sw[src], 0.0)[:, None]
    x_sorted = jnp.take(h, row_tok, axis=0)
    y_sorted = _grouped_ffn(x_sorted, row_w, block_e, n_used, w1, w3, w2)
    dest = pstart[idx] + jnp.take_along_axis(csum, idx, axis=1) - 1
    return jnp.sum(jnp.take(y_sorted, dest, axis=0), axis=1)


def _rope_tables(pos):
    half = ROT_DIM // 2
    inv = jnp.power(ROPE_THETA, -jnp.arange(half, dtype=F32) / half)
    ang = pos.astype(F32)[:, None] * inv[None, :]
    cos, sin = jnp.cos(ang), jnp.sin(ang)
    n = pos.shape[0]
    ones = jnp.ones((n, HEAD_DIM - ROT_DIM), F32)
    zeros = jnp.zeros((n, HEAD_DIM - half), F32)
    cos_t = jnp.concatenate([cos, cos, ones], axis=1)
    sin_a = jnp.concatenate([-sin, zeros], axis=1)
    sin_b = jnp.concatenate([jnp.zeros((n, half), F32), sin, jnp.zeros((n, HEAD_DIM - ROT_DIM), F32)], axis=1)
    return cos_t, sin_a, sin_b


def kernel(x_prompt, x_sample, c_prompt, c_sample, cache_k, cache_v, state_ssm, state_conv, page_table, w_ada, b_ada, norm_mix_pre, norm_mix_post, norm_ffn_pre, norm_ffn_post, w_in, conv_w, a_log, dt_bias, gdn_norm, w_out, w_router, e_bias, w1, w3, w2, ws1, ws3, ws2):
    B, S, D = x_prompt.shape
    Bs = x_sample.shape[0]
    assert x_sample.shape[1] == 1 and w_ada.shape[0] == 1
    assert Bs == ROW_BLK and S % MOBA_BLOCK == 0 and S % ROW_BLK == 0
    Tp = B * S
    T = Tp + Bs
    rb_per_seq = S // ROW_BLK
    past_len = page_table.shape[1] * PAGE_SIZE
    H = GDN_HEADS

    x_all = jnp.concatenate([x_prompt.reshape(Tp, D), x_sample.reshape(Bs, D)], axis=0)

    c_all = jnp.concatenate([c_prompt, c_sample], axis=0)
    n_c = c_all.shape[0]
    c_pad = jnp.pad(c_all, ((0, (-n_c) % 8), (0, 0)))
    mod = _matmul(c_pad, w_ada[0], b_ada, silu_in=True, tm=c_pad.shape[0])
    mod_rows = jnp.concatenate([jnp.repeat(mod[:B], ROW_BLK, axis=0), mod[B:B + Bs]], axis=0)

    h = _prenorm(x_all, norm_mix_pre, mod_rows, 0, 1, rb_per_seq)
    w_in0 = w_in[0]
    proj = _matmul(h, w_in0[:, :OFF_AB])
    w_ab = jnp.pad(w_in0[:, OFF_AB:], ((0, 0), (0, HEAD_DIM - 2 * H)))
    ab = _matmul(h, w_ab)

    pos = jnp.concatenate([jnp.tile(jnp.arange(S, dtype=jnp.int32), B),
                           jnp.full((Bs,), past_len, jnp.int32)])
    qk_rot = _rope(proj, *_rope_tables(pos))
    k_rot = qk_rot[:, Q_ATTN:]
    v_new = proj[:, OFF_V:OFF_V + KV_ATTN]
    o_a_p = _attn_prefill(qk_rot, proj, B, S)
    grp = ATTN_HEADS // KV_HEADS
    q_s = qk_rot[Tp:, :Q_ATTN].reshape(Bs, ATTN_HEADS, HEAD_DIM)
    kx = jnp.repeat(k_rot[Tp:].reshape(Bs, KV_HEADS, HEAD_DIM), grp, axis=1)
    vx = jnp.repeat(v_new[Tp:].reshape(Bs, KV_HEADS, HEAD_DIM), grp, axis=1)
    n_pool = cache_k.shape[1]
    o_a_s = _attn_decode(q_s, kx, vx, cache_k[0].reshape(n_pool, PAGE_SIZE, KV_ATTN),
                         cache_v[0].reshape(n_pool, PAGE_SIZE, KV_ATTN), page_table)
    o_a = jnp.concatenate([o_a_p, o_a_s.reshape(Bs, Q_ATTN)], axis=0)

    alog_pad = jnp.pad(a_log.astype(F32), ((0, 0), (0, HEAD_DIM - H)))
    dtb_pad = jnp.pad(dt_bias.astype(F32), ((0, 0), (0, HEAD_DIM - H)))
    gb = _gates(ab, alog_pad, dtb_pad)
    qkv_p = _conv_prompt(proj, conv_w[0], B, S)
    buf_t = state_conv[0].transpose(1, 0, 2)
    qkv_s = _conv_sample(proj, buf_t, conv_w[0], Tp // Bs)
    n_chunks = S // GDN_CHUNK

    def chunk_rows(a):
        return a.reshape(B, n_chunks, GDN_CHUNK, H).transpose(0, 1, 3, 2)

    o_g_p, ssm_p = _gdn_prompt(qkv_p, chunk_rows(gb[:Tp, :H]), chunk_rows(gb[:Tp, H:2 * H]), B, S)
    o_g_s, ssm_s = _gdn_sample(qkv_s.reshape(Bs, 3 * H, HEAD_DIM), gb[Tp:].reshape(Bs, 1, HEAD_DIM),
                               state_ssm[0].astype(F32))
    o_g = jnp.concatenate([o_g_p, o_g_s.reshape(Bs, V_GDN)], axis=0)
    o_gn = _gated_norm(o_g, proj, gdn_norm)

    mix = _matmul(jnp.concatenate([o_a, o_gn], axis=1), w_out[0])

    x1, h2, logits = _mid(x_all, mix, norm_mix_post, norm_ffn_pre, mod_rows, w_router[0], rb_per_seq)
    idx_l, wts_l = _route(logits, e_bias)
    routed = _moe_routed(h2, idx_l[:, :TOP_K], wts_l[:, :TOP_K],
                         w1[0].astype(BF16), w3[0].astype(BF16), w2[0].astype(BF16))
    shared = _shared_ffn(h2, ws1[0].astype(BF16), ws3[0].astype(BF16), ws2[0].astype(BF16))
    y_all = _final(x1, routed, shared, norm_ffn_post, mod_rows, rb_per_seq)

    y_prompt = y_all[:Tp].reshape(B, S, D)
    y_sample = y_all[Tp:].reshape(Bs, 1, D)
    k_prompt = k_rot[:Tp].reshape(1, B, S, KV_HEADS, HEAD_DIM)
    v_prompt = v_new[:Tp].reshape(1, B, S, KV_HEADS, HEAD_DIM)
    k_sample = k_rot[Tp:].reshape(1, Bs, 1, KV_HEADS, HEAD_DIM)
    v_sample = v_new[Tp:].reshape(1, Bs, 1, KV_HEADS, HEAD_DIM)
    pre_conv = proj[:, OFF_CONV:OFF_CONV + GDN_CONV_CH]
    conv_prompt = pre_conv[:Tp].reshape(B, S, GDN_CONV_CH)[:, S - (CONV_W - 1):][None]
    conv_sample = jnp.concatenate([state_conv[0][:, 1:], pre_conv[Tp:][:, None, :]], axis=1)[None]
    return (y_prompt, y_sample, k_prompt, v_prompt, k_sample, v_sample,
            ssm_p[None].astype(state_ssm.dtype), ssm_s[None].astype(state_ssm.dtype),
            conv_prompt.astype(state_conv.dtype), conv_sample.astype(state_conv.dtype))
```

```python
import functools

import jax
import jax.numpy as jnp
from jax import lax
from jax.experimental import pallas as pl
from jax.experimental.pallas import tpu as pltpu

F32 = jnp.float32
BF16 = jnp.bfloat16
HIGHEST = lax.Precision.HIGHEST

HEAD_DIM = 128
ATTN_HEADS = 16
KV_HEADS = 4
GDN_HEADS = 16
CONV_W = 4
GDN_CHUNK = 64
MOBA_BLOCK = 256
MOBA_TOPK = 3
ROPE_THETA = 500000.0
ROT_DIM = HEAD_DIM // 4
PAGE_SIZE = 128
N_GROUPS = 8
TOPK_GROUPS = 4
TOP_K = 8
ROUTED_SCALE = 2.5
NORM_EPS = 1e-6

Q_ATTN = ATTN_HEADS * HEAD_DIM
KV_ATTN = KV_HEADS * HEAD_DIM
QK_GDN = GDN_HEADS * HEAD_DIM
V_GDN = GDN_HEADS * HEAD_DIM
GDN_CONV_CH = 2 * QK_GDN + V_GDN
OFF_K = Q_ATTN
OFF_V = Q_ATTN + KV_ATTN
OFF_CONV = Q_ATTN + 2 * KV_ATTN
OFF_Z = OFF_CONV + GDN_CONV_CH
OFF_AB = OFF_Z + V_GDN

ROW_BLK = 128
MOE_BLK = 512
MOE_SUB = 256
NEG = -1e30
VMEM_LIMIT = 56 << 20


def _params(*sem):
    return pltpu.CompilerParams(dimension_semantics=sem, vmem_limit_bytes=VMEM_LIMIT)


def _pick(n, prefs):
    for p in prefs:
        if n % p == 0:
            return p
    return n


def _nt(a, b, **kw):
    return lax.dot_general(a, b, (((1,), (1,)), ((), ())), preferred_element_type=F32, **kw)


def _silu(x):
    return x * jax.nn.sigmoid(x)


def _rms(x, g):
    return x * lax.rsqrt(jnp.mean(x * x, axis=-1, keepdims=True) + NORM_EPS) * g


def _beats(a, b, a_first):
    return jnp.where(a > b, 1.0, jnp.where(a == b, jnp.where(a_first, 1.0, 0.0), 0.0))


def _mm_body(x_ref, w_ref, b_ref, o_ref, wbf_ref, *, silu_in):
    @pl.when(pl.program_id(1) == 0)
    def _cast_weight_tile():
        wbf_ref[...] = w_ref[...].astype(BF16)

    x = x_ref[...]
    if silu_in:
        x = _silu(x.astype(F32))
    acc = jnp.dot(x.astype(BF16), wbf_ref[...], preferred_element_type=F32)
    o_ref[...] = (acc + b_ref[...]).astype(o_ref.dtype)


def _matmul(x, w, b=None, *, name, n_cols=None, silu_in=False, out_dtype=F32, tm=None, tn=None):
    M, K = x.shape
    N = n_cols or w.shape[1]
    tm = tm or _pick(M, (640, 512, 256, 128))
    tn = tn or _pick(N, (512, 256, 128))
    if b is None:
        b = jnp.zeros((1, N), F32)
    return pl.pallas_call(
        functools.partial(_mm_body, silu_in=silu_in),
        name=name,
        grid=(N // tn, M // tm),
        in_specs=[
            pl.BlockSpec((tm, K), lambda n, m: (m, 0)),
            pl.BlockSpec((K, tn), lambda n, m: (0, n)),
            pl.BlockSpec((1, tn), lambda n, m: (0, n)),
        ],
        out_specs=pl.BlockSpec((tm, tn), lambda n, m: (m, n)),
        out_shape=jax.ShapeDtypeStruct((M, N), out_dtype),
        scratch_shapes=[pltpu.VMEM((K, tn), BF16)],
        compiler_params=_params("arbitrary", "arbitrary"),
    )(x, w, b)


def _prenorm_body(x_ref, g_ref, sh_ref, sc_ref, o_ref):
    y = _rms(x_ref[...], g_ref[...]) * (1.0 + sc_ref[...]) + sh_ref[...]
    o_ref[...] = y.astype(o_ref.dtype)


def _mod_spec(D, slot, rb_per_seq):
    return pl.BlockSpec((ROW_BLK, D), lambda r: (r // rb_per_seq, slot))


def _row_spec(D):
    return pl.BlockSpec((ROW_BLK, D), lambda r: (r, 0))


def _vec_spec(D):
    return pl.BlockSpec((1, D), lambda r: (0, 0))


def _prenorm(x, g, mod, slot_sh, slot_sc, rb_per_seq):
    T, D = x.shape
    return pl.pallas_call(
        _prenorm_body,
        name="prenorm",
        grid=(T // ROW_BLK,),
        in_specs=[_row_spec(D), _vec_spec(D), _mod_spec(D, slot_sh, rb_per_seq),
                  _mod_spec(D, slot_sc, rb_per_seq)],
        out_specs=_row_spec(D),
        out_shape=jax.ShapeDtypeStruct((T, D), BF16),
        compiler_params=_params("arbitrary"),
    )(x, g, mod, mod)


def _mid_body(x_ref, mix_ref, gpost_ref, gt_ref, gpre_ref, sh_ref, sc_ref, wr_ref,
              x1_ref, h_ref, lg_ref):
    x1 = x_ref[...] + gt_ref[...] * _rms(mix_ref[...], gpost_ref[...])
    x1_ref[...] = x1
    h = _rms(x1, gpre_ref[...]) * (1.0 + sc_ref[...]) + sh_ref[...]
    h_ref[...] = h.astype(BF16)
    lg_ref[...] = jnp.dot(h, wr_ref[...], precision=HIGHEST, preferred_element_type=F32)


def _mid(x, mix, g_post, g_pre, mod, w_router, rb_per_seq):
    T, D = x.shape
    E = w_router.shape[1]
    return pl.pallas_call(
        _mid_body,
        name="mid_norms_router",
        grid=(T // ROW_BLK,),
        in_specs=[_row_spec(D), _row_spec(D), _vec_spec(D), _mod_spec(D, 2, rb_per_seq),
                  _vec_spec(D), _mod_spec(D, 3, rb_per_seq), _mod_spec(D, 4, rb_per_seq),
                  pl.BlockSpec((D, E), lambda r: (0, 0))],
        out_specs=[_row_spec(D), _row_spec(D), _row_spec(E)],
        out_shape=[jax.ShapeDtypeStruct((T, D), F32), jax.ShapeDtypeStruct((T, D), BF16),
                   jax.ShapeDtypeStruct((T, E), F32)],
        compiler_params=_params("arbitrary"),
    )(x, mix, g_post, mod, g_pre, mod, mod, w_router)


def _final_body(x1_ref, a_ref, b_ref, g_ref, gt_ref, o_ref):
    f = a_ref[...] + b_ref[...]
    o_ref[...] = x1_ref[...] + gt_ref[...] * _rms(f, g_ref[...])


def _final(x1, routed, shared, g_post, mod, rb_per_seq):
    T, D = x1.shape
    return pl.pallas_call(
        _final_body,
        name="final_norm",
        grid=(T // ROW_BLK,),
        in_specs=[_row_spec(D), _row_spec(D), _row_spec(D), _vec_spec(D),
                  _mod_spec(D, 5, rb_per_seq)],
        out_specs=_row_spec(D),
        out_shape=jax.ShapeDtypeStruct((T, D), F32),
        compiler_params=_params("arbitrary"),
    )(x1, routed, shared, g_post, mod)


def _rope_body(x_ref, c_ref, sa_ref, sb_ref, o_ref):
    x = x_ref[...]
    half = ROT_DIM // 2
    o_ref[...] = (x * c_ref[...] + pltpu.roll(x, HEAD_DIM - half, 1) * sa_ref[...]
                  + pltpu.roll(x, half, 1) * sb_ref[...])


def _rope(proj, cos_t, sin_a, sin_b):
    T = proj.shape[0]
    n_heads = ATTN_HEADS + KV_HEADS
    tr = _pick(T, (640, 512, 256, 128))
    tab = pl.BlockSpec((tr, HEAD_DIM), lambda r, h: (r, 0))
    blk = pl.BlockSpec((tr, HEAD_DIM), lambda r, h: (r, h))
    return pl.pallas_call(
        _rope_body,
        name="rope",
        grid=(T // tr, n_heads),
        in_specs=[blk, tab, tab, tab],
        out_specs=blk,
        out_shape=jax.ShapeDtypeStruct((T, n_heads * HEAD_DIM), F32),
        compiler_params=_params("arbitrary", "arbitrary"),
    )(proj, cos_t, sin_a, sin_b)


def _topk_mask(gate, lane, n_valid, n_cand):
    gate = jnp.where(lane < n_valid, gate, -jnp.inf)
    rank = jnp.zeros(gate.shape, F32)
    for m in range(n_cand):
        rank = rank + _beats(gate[:, m:m + 1], gate, m < lane)
    return jnp.where(lane < n_valid, jnp.where(rank < MOBA_TOPK, 1.0, 0.0), 0.0)


def _attn_prefill_body(q_ref, k_ref, v_ref, o_ref, m_ref, l_ref, acc_ref, sel_ref, *, nblk, grp):
    i = pl.program_id(2)
    blk = MOBA_BLOCK
    rows = grp * blk
    scale = HEAD_DIM ** -0.5
    qf = jnp.concatenate([q_ref[:, h * HEAD_DIM:(h + 1) * HEAD_DIM] for h in range(grp)], axis=0)
    qb = qf.astype(BF16)

    krow = lax.broadcasted_iota(jnp.int32, (HEAD_DIM, HEAD_DIM), 0)
    kmean = jnp.zeros((HEAD_DIM, HEAD_DIM), F32)
    for n in range(nblk):
        kmean = jnp.where(krow == n, jnp.mean(k_ref[n * blk:(n + 1) * blk, :], axis=0, keepdims=True), kmean)
    gate = _nt(qf, kmean, precision=HIGHEST)
    lane = lax.broadcasted_iota(jnp.int32, (rows, HEAD_DIM), 1)
    sel_ref[...] = _topk_mask(gate, lane, i, nblk)

    start = pl.multiple_of(i * blk, blk)
    s = _nt(qb, k_ref[pl.ds(start, blk), :].astype(BF16)) * scale
    qpos = jnp.bitwise_and(lax.broadcasted_iota(jnp.int32, (rows, blk), 0), blk - 1)
    kpos = lax.broadcasted_iota(jnp.int32, (rows, blk), 1)
    s = jnp.where(kpos <= qpos, s, NEG)
    m0 = jnp.max(s, axis=1, keepdims=True)
    p = jnp.exp(s - m0)
    m_ref[...] = m0
    l_ref[...] = jnp.sum(p, axis=1, keepdims=True)
    acc_ref[...] = jnp.dot(p.astype(BF16), v_ref[pl.ds(start, blk), :].astype(BF16),
                           preferred_element_type=F32)

    for n in range(nblk - 1):
        @pl.when(n < i)
        def _past_block(n=n):
            s = _nt(qb, k_ref[n * blk:(n + 1) * blk, :].astype(BF16)) * scale
            s = jnp.where(sel_ref[:, n:n + 1] > 0.0, s, NEG)
            m_old = m_ref[...]
            m_new = jnp.maximum(m_old, jnp.max(s, axis=1, keepdims=True))
            alpha = jnp.exp(m_old - m_new)
            p = jnp.exp(s - m_new)
            l_ref[...] = alpha * l_ref[...] + jnp.sum(p, axis=1, keepdims=True)
            acc_ref[...] = alpha * acc_ref[...] + jnp.dot(
                p.astype(BF16), v_ref[n * blk:(n + 1) * blk, :].astype(BF16),
                preferred_element_type=F32)
            m_ref[...] = m_new

    out = acc_ref[...] / l_ref[...]
    for h in range(grp):
        o_ref[:, h * HEAD_DIM:(h + 1) * HEAD_DIM] = out[h * blk:(h + 1) * blk].astype(o_ref.dtype)


def _attn_prefill(qk_rot, proj, B, S):
    nblk = S // MOBA_BLOCK
    grp = ATTN_HEADS // KV_HEADS
    rows = grp * MOBA_BLOCK
    k_col0 = OFF_K // HEAD_DIM
    v_col0 = OFF_V // HEAD_DIM
    return pl.pallas_call(
        functools.partial(_attn_prefill_body, nblk=nblk, grp=grp),
        name="attn_prefill",
        grid=(B, KV_HEADS, nblk),
        in_specs=[
            pl.BlockSpec((MOBA_BLOCK, grp * HEAD_DIM), lambda b, g, i: (b * nblk + i, g)),
            pl.BlockSpec((S, HEAD_DIM), lambda b, g, i: (b, k_col0 + g)),
            pl.BlockSpec((S, HEAD_DIM), lambda b, g, i: (b, v_col0 + g)),
        ],
        out_specs=pl.BlockSpec((MOBA_BLOCK, grp * HEAD_DIM), lambda b, g, i: (b * nblk + i, g)),
        out_shape=jax.ShapeDtypeStruct((B * S, Q_ATTN), BF16),
        scratch_shapes=[pltpu.VMEM((rows, 1), F32), pltpu.VMEM((rows, 1), F32),
                        pltpu.VMEM((rows, HEAD_DIM), F32), pltpu.VMEM((rows, HEAD_DIM), F32)],
        compiler_params=_params("arbitrary", "arbitrary", "arbitrary"),
    )(qk_rot, qk_rot, proj)


def _attn_decode_body(pt_ref, q_ref, kx_ref, vx_ref, *refs, n_pages):
    del pt_ref
    k_pages, v_pages, o_ref = refs[:n_pages], refs[n_pages:2 * n_pages], refs[2 * n_pages]
    H = ATTN_HEADS
    grp = ATTN_HEADS // KV_HEADS
    ppb = MOBA_BLOCK // PAGE_SIZE
    nblk = n_pages // ppb
    scale = HEAD_DIM ** -0.5
    q = q_ref[...]
    qb = q.astype(BF16)
    row = lax.broadcasted_iota(jnp.int32, (H, 1), 0)
    lane = lax.broadcasted_iota(jnp.int32, (H, HEAD_DIM), 1)
    s_own = jnp.sum(q * kx_ref[...], axis=1, keepdims=True) * scale
    vx = vx_ref[...]
    out = jnp.zeros((H, HEAD_DIM), F32)
    for h in range(KV_HEADS):
        kp = [k_pages[j][:, h, :] for j in range(n_pages)]
        gate = jnp.zeros((H, HEAD_DIM), F32)
        for n in range(nblk):
            ksum = kp[n * ppb].sum(axis=0, keepdims=True)
            for j in range(1, ppb):
                ksum = ksum + kp[n * ppb + j].sum(axis=0, keepdims=True)
            g_n = jnp.sum(q * (ksum / MOBA_BLOCK), axis=1, keepdims=True)
            gate = jnp.where(lane == n, g_n, gate)
        sel = _topk_mask(gate, lane, nblk, nblk)
        kf = jnp.concatenate(kp, axis=0).astype(BF16)
        s = _nt(qb, kf) * scale
        keymask = jnp.concatenate(
            [jnp.broadcast_to(sel[:, n:n + 1], (H, MOBA_BLOCK)) for n in range(nblk)], axis=1)
        s = jnp.where(keymask > 0.0, s, NEG)
        m = jnp.maximum(jnp.max(s, axis=1, keepdims=True), s_own)
        p = jnp.exp(s - m)
        e_own = jnp.exp(s_own - m)
        denom = jnp.sum(p, axis=1, keepdims=True) + e_own
        vf = jnp.concatenate([v_pages[j][:, h, :] for j in range(n_pages)], axis=0).astype(BF16)
        o_h = (jnp.dot(p.astype(BF16), vf, preferred_element_type=F32) + e_own * vx) / denom
        mine = jnp.logical_and(row >= h * grp, row < (h + 1) * grp)
        out = jnp.where(mine, o_h, out)
    o_ref[...] = out.astype(o_ref.dtype)


def _attn_decode(q_s, kx, vx, cache_k, cache_v, page_table):
    Bs, n_pages = page_table.shape
    assert n_pages % (MOBA_BLOCK // PAGE_SIZE) == 0
    vec = pl.BlockSpec((None, ATTN_HEADS, HEAD_DIM), lambda b, pt: (b, 0, 0))

    def page(j):
        return pl.BlockSpec((None, None, PAGE_SIZE, KV_HEADS, HEAD_DIM),
                            lambda b, pt: (0, pt[b, j], 0, 0, 0))

    pages = [page(j) for j in range(n_pages)]
    return pl.pallas_call(
        functools.partial(_attn_decode_body, n_pages=n_pages),
        name="attn_decode",
        grid_spec=pltpu.PrefetchScalarGridSpec(
            num_scalar_prefetch=1,
            grid=(Bs,),
            in_specs=[vec, vec, vec] + pages + pages,
            out_specs=vec,
        ),
        out_shape=jax.ShapeDtypeStruct((Bs, ATTN_HEADS, HEAD_DIM), BF16),
        compiler_params=_params("arbitrary"),
    )(page_table, q_s, kx, vx, *([cache_k] * n_pages), *([cache_v] * n_pages))


def _conv_finish(y, c):
    y = _silu(y)
    nrm = y * lax.rsqrt(jnp.sum(y * y, axis=1, keepdims=True) + 1e-6)
    return jnp.where(c < GDN_HEADS, nrm * (HEAD_DIM ** -0.5), jnp.where(c < 2 * GDN_HEADS, nrm, y))


def _conv_prompt_body(x_ref, w_ref, o_ref):
    x = x_ref[...]
    row = lax.broadcasted_iota(jnp.int32, x.shape, 0)
    y = x * w_ref[CONV_W - 1:CONV_W, :]
    for k in range(1, CONV_W):
        y = y + jnp.where(row >= k, pltpu.roll(x, k, 0), 0.0) * w_ref[CONV_W - 1 - k:CONV_W - k, :]
    o_ref[...] = _conv_finish(y, pl.program_id(1))


def _conv_prompt(proj, conv_w, B, S):
    n_tiles = GDN_CONV_CH // HEAD_DIM
    col0 = OFF_CONV // HEAD_DIM
    return pl.pallas_call(
        _conv_prompt_body,
        name="conv_prompt",
        grid=(B, n_tiles),
        in_specs=[pl.BlockSpec((S, HEAD_DIM), lambda b, c: (b, col0 + c)),
                  pl.BlockSpec((CONV_W, HEAD_DIM), lambda b, c: (0, c))],
        out_specs=pl.BlockSpec((S, HEAD_DIM), lambda b, c: (b, c)),
        out_shape=jax.ShapeDtypeStruct((B * S, GDN_CONV_CH), F32),
        compiler_params=_params("arbitrary", "arbitrary"),
    )(proj, conv_w)


def _conv_sample_body(x_ref, buf_ref, w_ref, o_ref):
    y = x_ref[...] * w_ref[CONV_W - 1:CONV_W, :]
    for i in range(CONV_W - 1):
        y = y + buf_ref[i] * w_ref[i:i + 1, :]
    o_ref[...] = _conv_finish(y, pl.program_id(0))


def _conv_sample(proj, buf_t, conv_w, row_blk0):
    Bs = buf_t.shape[1]
    n_tiles = GDN_CONV_CH // HEAD_DIM
    col0 = OFF_CONV // HEAD_DIM
    return pl.pallas_call(
        _conv_sample_body,
        name="conv_sample",
        grid=(n_tiles,),
        in_specs=[pl.BlockSpec((Bs, HEAD_DIM), lambda c: (row_blk0, col0 + c)),
                  pl.BlockSpec((CONV_W - 1, Bs, HEAD_DIM), lambda c: (0, 0, c)),
                  pl.BlockSpec((CONV_W, HEAD_DIM), lambda c: (0, c))],
        out_specs=pl.BlockSpec((Bs, HEAD_DIM), lambda c: (0, c)),
        out_shape=jax.ShapeDtypeStruct((Bs, GDN_CONV_CH), F32),
        compiler_params=_params("arbitrary"),
    )(proj, buf_t, conv_w)


def _gates_body(ab_ref, alog_ref, dtb_ref, o_ref):
    x = ab_ref[...]
    sp_in = x + dtb_ref[...]
    softplus = jnp.maximum(sp_in, 0.0) + jnp.log1p(jnp.exp(-jnp.abs(sp_in)))
    g = -jnp.exp(alog_ref[...]) * softplus
    lane = lax.broadcasted_iota(jnp.int32, x.shape, 1)
    o_ref[...] = jnp.where(lane < GDN_HEADS, g, jax.nn.sigmoid(x))


def _gates(ab, alog_pad, dtb_pad):
    T = ab.shape[0]
    return pl.pallas_call(
        _gates_body,
        name="gdn_gates",
        grid=(T // ROW_BLK,),
        in_specs=[_row_spec(HEAD_DIM), _vec_spec(HEAD_DIM), _vec_spec(HEAD_DIM)],
        out_specs=_row_spec(HEAD_DIM),
        out_shape=jax.ShapeDtypeStruct((T, HEAD_DIM), F32),
        compiler_params=_params("arbitrary"),
    )(ab, alog_pad, dtb_pad)


def _mmb(a, b):
    return jnp.dot(a.astype(BF16), b.astype(BF16), preferred_element_type=F32)


def _split_bf16(a):
    hi = a.astype(BF16)
    return hi, (a - hi.astype(F32)).astype(BF16)


def _mm3(a, b):
    ah, al = _split_bf16(a)
    bh, bl = _split_bf16(b)

    def d(x, y):
        return jnp.dot(x, y, preferred_element_type=F32)

    return d(ah, bh) + (d(ah, bl) + d(al, bh))


def _unit_lower_inverse(lows):
    C = lows[0].shape[0]
    ii = lax.broadcasted_iota(jnp.int32, (C, C), 0)
    jj = lax.broadcasted_iota(jnp.int32, (C, C), 1)
    eye = jnp.where(ii == jj, 1.0, 0.0)
    invs = [eye - low for low in lows]
    pws = [_mm3(low, low) for low in lows]
    span = 2
    while span < C:
        invs = [inv + _mm3(inv, pw) for inv, pw in zip(invs, pws)]
        span *= 2
        if span < C:
            pws = [_mm3(pw, pw) for pw in pws]
    return invs


GDN_HEAD_GROUP = 8


def _gdn_chunk_body(q_ref, k_ref, v_ref, g_ref, b_ref, o_ref, s_out_ref, s_ref):
    n = pl.program_id(1)
    C = GDN_CHUNK

    @pl.when(n == 0)
    def _zero_state():
        s_ref[...] = jnp.zeros_like(s_ref)

    ii = lax.broadcasted_iota(jnp.int32, (C, C), 0)
    jj = lax.broadcasted_iota(jnp.int32, (C, C), 1)
    eye = ii == jj
    tril = jj <= ii
    for h0 in range(0, GDN_HEADS, GDN_HEAD_GROUP):
        hs = range(h0, h0 + GDN_HEAD_GROUP)
        cols = [slice(h * HEAD_DIM, (h + 1) * HEAD_DIM) for h in hs]
        q = [q_ref[:, c] for c in cols]
        k = [k_ref[:, c] for c in cols]
        v = [v_ref[:, c] for c in cols]
        gc_col = [jnp.sum(jnp.where(tril, g_ref[h:h + 1, :], 0.0), axis=1, keepdims=True) for h in hs]
        gc_row = [jnp.sum(jnp.where(eye, x, 0.0), axis=0, keepdims=True) for x in gc_col]
        beta = [jnp.sum(jnp.where(eye, b_ref[h:h + 1, :], 0.0), axis=1, keepdims=True) for h in hs]
        decay = [jnp.exp(jnp.where(tril, c_ - r_, -jnp.inf)) for c_, r_ in zip(gc_col, gc_row)]
        g_last = [x[C - 1:C, :] for x in gc_col]
        e_gc = [jnp.exp(x) for x in gc_col]
        kbeta = [k_ * b_ for k_, b_ in zip(k, beta)]
        k16 = [x.astype(BF16) for x in k]
        low = [jnp.where(ii > jj, _nt(kb.astype(BF16), kk) * d, 0.0) for kb, kk, d in zip(kbeta, k16, decay)]
        tmat = [t.astype(BF16) for t in _unit_lower_inverse(low)]
        u = [_mmb(t, v_ * b_) for t, v_, b_ in zip(tmat, v, beta)]
        w = [_mmb(t, kb * e) for t, kb, e in zip(tmat, kbeta, e_gc)]
        attn = [_nt(q_.astype(BF16), kk) * d for q_, kk, d in zip(q, k16, decay)]
        state = [s_ref[h] for h in hs]
        v_new = [u_ - _mmb(w_, s_) for u_, w_, s_ in zip(u, w, state)]
        for c, q_, e, s_, a_, vn in zip(cols, q, e_gc, state, attn, v_new):
            o_ref[:, c] = _mmb(q_ * e, s_) + _mmb(a_, vn)
        for h, k_, gl, gc, s_, vn in zip(hs, k, g_last, gc_col, state, v_new):
            k_tail = k_ * jnp.exp(gl - gc)
            s_ref[h] = s_ * jnp.exp(gl) + _mmb(k_tail.T, vn)

    @pl.when(n == pl.num_programs(1) - 1)
    def _emit_state():
        s_out_ref[...] = s_ref[...]


def _gdn_prompt(qkv, g_rows, b_rows, B, S):
    N = S // GDN_CHUNK
    W = GDN_HEADS * HEAD_DIM

    def part(j):
        return pl.BlockSpec((GDN_CHUNK, W), lambda b, n: (b * N + n, j))

    gspec = pl.BlockSpec((None, None, GDN_HEADS, GDN_CHUNK), lambda b, n: (b, n, 0, 0))
    return pl.pallas_call(
        _gdn_chunk_body,
        name="gdn_prompt",
        grid=(B, N),
        in_specs=[part(0), part(1), part(2), gspec, gspec],
        out_specs=[pl.BlockSpec((GDN_CHUNK, W), lambda b, n: (b * N + n, 0)),
                   pl.BlockSpec((None, GDN_HEADS, HEAD_DIM, HEAD_DIM), lambda b, n: (b, 0, 0, 0))],
        out_shape=[jax.ShapeDtypeStruct((B * S, W), F32),
                   jax.ShapeDtypeStruct((B, GDN_HEADS, HEAD_DIM, HEAD_DIM), F32)],
        scratch_shapes=[pltpu.VMEM((GDN_HEADS, HEAD_DIM, HEAD_DIM), F32)],
        compiler_params=_params("arbitrary", "arbitrary"),
    )(qkv, qkv, qkv, g_rows, b_rows)


def _gdn_step_body(qkv_ref, gb_ref, s_ref, o_ref, s_out_ref):
    H = GDN_HEADS
    ii = lax.broadcasted_iota(jnp.int32, (HEAD_DIM, HEAD_DIM), 0)
    jj = lax.broadcasted_iota(jnp.int32, (HEAD_DIM, HEAD_DIM), 1)
    eye = ii == jj
    lane = lax.broadcasted_iota(jnp.int32, (1, HEAD_DIM), 1)
    gb = gb_ref[...]
    for h in range(H):
        q_col = jnp.sum(jnp.where(eye, qkv_ref[h:h + 1, :], 0.0), axis=1, keepdims=True)
        k_col = jnp.sum(jnp.where(eye, qkv_ref[H + h:H + h + 1, :], 0.0), axis=1, keepdims=True)
        v_row = qkv_ref[2 * H + h:2 * H + h + 1, :]
        g = jnp.sum(jnp.where(lane == h, gb, 0.0), axis=1, keepdims=True)
        beta = jnp.sum(jnp.where(lane == H + h, gb, 0.0), axis=1, keepdims=True)
        state = s_ref[h] * jnp.exp(g)
        kv = jnp.sum(k_col * state, axis=0, keepdims=True)
        delta = (v_row - kv) * beta
        state = state + k_col * delta
        s_out_ref[h] = state
        o_ref[h:h + 1, :] = jnp.sum(q_col * state, axis=0, keepdims=True)


def _gdn_sample(qkv3, gb3, state):
    Bs = qkv3.shape[0]
    H = GDN_HEADS
    sspec = pl.BlockSpec((None, H, HEAD_DIM, HEAD_DIM), lambda b: (b, 0, 0, 0))
    return pl.pallas_call(
        _gdn_step_body,
        name="gdn_sample",
        grid=(Bs,),
        in_specs=[pl.BlockSpec((None, 3 * H, HEAD_DIM), lambda b: (b, 0, 0)),
                  pl.BlockSpec((None, 1, HEAD_DIM), lambda b: (b, 0, 0)), sspec],
        out_specs=[pl.BlockSpec((None, H, HEAD_DIM), lambda b: (b, 0, 0)), sspec],
        out_shape=[jax.ShapeDtypeStruct((Bs, H, HEAD_DIM), F32),
                   jax.ShapeDtypeStruct(state.shape, F32)],
        compiler_params=_params("arbitrary"),
    )(qkv3, gb3, state)


def _gated_norm_body(o_ref, z_ref, g_ref, out_ref):
    for h in range(o_ref.shape[1] // HEAD_DIM):
        c = slice(h * HEAD_DIM, (h + 1) * HEAD_DIM)
        out_ref[:, c] = (_rms(o_ref[:, c], g_ref[...]) * _silu(z_ref[:, c])).astype(out_ref.dtype)


def _gated_norm(o_g, proj, gdn_norm):
    T = o_g.shape[0]
    W = 8 * HEAD_DIM
    assert OFF_Z % W == 0 and V_GDN % W == 0
    tm = _pick(T, (640, 512, 256, 128))
    z_col0 = OFF_Z // W
    return pl.pallas_call(
        _gated_norm_body,
        name="gdn_gated_norm",
        grid=(T // tm, V_GDN // W),
        in_specs=[pl.BlockSpec((tm, W), lambda r, c: (r, c)),
                  pl.BlockSpec((tm, W), lambda r, c: (r, z_col0 + c)),
                  pl.BlockSpec((1, HEAD_DIM), lambda r, c: (0, 0))],
        out_specs=pl.BlockSpec((tm, W), lambda r, c: (r, c)),
        out_shape=jax.ShapeDtypeStruct((T, V_GDN), BF16),
        compiler_params=_params("arbitrary", "arbitrary"),
    )(o_g, proj, gdn_norm)


def _route_body(lg_ref, bias_ref, idx_ref, wt_ref):
    scores = jax.nn.sigmoid(lg_ref[...])
    biased = scores + bias_ref[...]
    shape = scores.shape
    E = shape[1]
    per_grp = E // N_GROUPS
    lane_i = lax.broadcasted_iota(jnp.int32, shape, 1)
    lane = lane_i.astype(F32)
    ninf = -jnp.inf

    def in_grp(g):
        return jnp.logical_and(lane_i >= g * per_grp, lane_i < (g + 1) * per_grp)

    gscore = []
    for g in range(N_GROUPS):
        xg = jnp.where(in_grp(g), biased, ninf)
        m1 = jnp.max(xg, axis=1, keepdims=True)
        i1 = jnp.min(jnp.where(xg == m1, lane, float(E)), axis=1, keepdims=True)
        m2 = jnp.max(jnp.where(lane == i1, ninf, xg), axis=1, keepdims=True)
        gscore.append(m1 + m2)
    keep = jnp.zeros(shape, F32)
    for g in range(N_GROUPS):
        rank = jnp.zeros_like(gscore[g])
        for o in range(N_GROUPS):
            if o != g:
                rank = rank + _beats(gscore[o], gscore[g], o < g)
        keep = jnp.where(in_grp(g), jnp.where(rank < TOPK_GROUPS, 1.0, 0.0), keep)

    cur = jnp.where(keep > 0.0, biased, ninf)
    idx = jnp.zeros(shape, F32)
    wts = jnp.zeros(shape, F32)
    wsum = jnp.zeros((shape[0], 1), F32)
    for k in range(TOP_K):
        mk = jnp.max(cur, axis=1, keepdims=True)
        ik = jnp.min(jnp.where(cur == mk, lane, float(E)), axis=1, keepdims=True)
        hit = lane == ik
        wk = jnp.sum(jnp.where(hit, scores, 0.0), axis=1, keepdims=True)
        cur = jnp.where(hit, ninf, cur)
        idx = jnp.where(lane_i == k, ik, idx)
        wts = jnp.where(lane_i == k, wk, wts)
        wsum = wsum + wk
    idx_ref[...] = idx.astype(jnp.int32)
    wt_ref[...] = wts / wsum * ROUTED_SCALE


def _route(logits, e_bias):
    T, E = logits.shape
    return pl.pallas_call(
        _route_body,
        name="route",
        grid=(T // ROW_BLK,),
        in_specs=[_row_spec(E), _vec_spec(E)],
        out_specs=[_row_spec(E), _row_spec(E)],
        out_shape=[jax.ShapeDtypeStruct((T, E), jnp.int32), jax.ShapeDtypeStruct((T, E), F32)],
        compiler_params=_params("arbitrary"),
    )(logits, e_bias)


def _ffn_body(x_ref, w1_ref, w3_ref, w2_ref, o_ref):
    x = x_ref[...]
    a = _silu(jnp.dot(x, w1_ref[...], preferred_element_type=F32))
    a = a * jnp.dot(x, w3_ref[...], preferred_element_type=F32)
    o_ref[...] = jnp.dot(a.astype(BF16), w2_ref[...], preferred_element_type=F32)


def _shared_ffn(x, w1, w3, w2):
    T, D = x.shape
    F = w1.shape[1]
    tm = _pick(T, (640, 512, 256, 128))
    return pl.pallas_call(
        _ffn_body,
        name="shared_ffn",
        grid=(T // tm,),
        in_specs=[pl.BlockSpec((tm, D), lambda r: (r, 0)),
                  pl.BlockSpec((D, F), lambda r: (0, 0)), pl.BlockSpec((D, F), lambda r: (0, 0)),
                  pl.BlockSpec((F, D), lambda r: (0, 0))],
        out_specs=pl.BlockSpec((tm, D), lambda r: (r, 0)),
        out_shape=jax.ShapeDtypeStruct((T, D), F32),
        compiler_params=_params("arbitrary"),
    )(x, w1, w3, w2)


def _grouped_ffn_body(be_ref, nu_ref, bv_ref, x_ref, rw_ref, w1_ref, w3_ref, w2_ref, o_ref,
                      h1_ref, h3_ref, act_ref):
    del be_ref
    i = pl.program_id(0)
    p = pl.program_id(1)
    used = i < nu_ref[0]
    n_valid = bv_ref[i]
    subs = [(s0, min(MOE_SUB, MOE_BLK - s0)) for s0 in range(0, MOE_BLK, MOE_SUB)]

    @pl.when(jnp.logical_and(used, p < 2))
    def _up():
        w1 = w1_ref[...].astype(BF16)
        w3 = w3_ref[...].astype(BF16)
        for s0, sz in subs:
            @pl.when(s0 < n_valid)
            def _sub(s0=s0, sz=sz):
                x = x_ref[s0:s0 + sz, :]
                a1 = jnp.dot(x, w1, preferred_element_type=F32)
                a3 = jnp.dot(x, w3, preferred_element_type=F32)

                @pl.when(p == 0)
                def _first():
                    h1_ref[s0:s0 + sz, :] = a1
                    h3_ref[s0:s0 + sz, :] = a3

                @pl.when(p == 1)
                def _second():
                    act_ref[s0:s0 + sz, :] = (_silu(h1_ref[s0:s0 + sz, :] + a1)
                                              * (h3_ref[s0:s0 + sz, :] + a3)).astype(BF16)

    @pl.when(jnp.logical_and(used, p >= 2))
    def _down():
        w2 = w2_ref[...].astype(BF16)
        for s0, sz in subs:
            @pl.when(s0 < n_valid)
            def _sub(s0=s0, sz=sz):
                y = jnp.dot(act_ref[s0:s0 + sz, :], w2, preferred_element_type=F32)
                o_ref[s0:s0 + sz, :] = y * rw_ref[s0:s0 + sz, :]

            @pl.when(s0 >= n_valid)
            def _pad(s0=s0, sz=sz):
                o_ref[s0:s0 + sz, :] = jnp.zeros((sz, o_ref.shape[1]), F32)

    @pl.when(jnp.logical_not(used))
    def _unused_block():
        o_ref[...] = jnp.zeros_like(o_ref)


def _grouped_ffn(x_sorted, row_w, block_e, n_used, block_valid, w1, w3, w2):
    R, D = x_sorted.shape
    E, _, F = w1.shape
    nb = R // MOE_BLK
    Dh = D // 2

    def last_used(i, nu):
        return jnp.minimum(i, nu[0] - 1)

    def k_half(i, p, nu):
        return jnp.where(i < nu[0], jnp.minimum(p, 1), 1)

    def n_half(i, p, nu):
        return jnp.where(i < nu[0], jnp.maximum(p - 2, 0), 1)

    return pl.pallas_call(
        _grouped_ffn_body,
        name="moe_grouped_ffn",
        grid_spec=pltpu.PrefetchScalarGridSpec(
            num_scalar_prefetch=3,
            grid=(nb, 4),
            in_specs=[
                pl.BlockSpec((MOE_BLK, Dh), lambda i, p, be, nu, bv: (last_used(i, nu), k_half(i, p, nu))),
                pl.BlockSpec((MOE_BLK, 1), lambda i, p, be, nu, bv: (last_used(i, nu), 0)),
                pl.BlockSpec((None, Dh, F), lambda i, p, be, nu, bv: (be[i], k_half(i, p, nu), 0)),
                pl.BlockSpec((None, Dh, F), lambda i, p, be, nu, bv: (be[i], k_half(i, p, nu), 0)),
                pl.BlockSpec((None, F, Dh), lambda i, p, be, nu, bv: (be[i], 0, n_half(i, p, nu))),
            ],
            out_specs=pl.BlockSpec((MOE_BLK, Dh),
                                   lambda i, p, be, nu, bv: (jnp.minimum(i, nu[0]),
                                                             jnp.where(i < nu[0], jnp.maximum(p - 2, 0), 0))),
            scratch_shapes=[pltpu.VMEM((MOE_BLK, F), F32), pltpu.VMEM((MOE_BLK, F), F32),
                            pltpu.VMEM((MOE_BLK, F), BF16)],
        ),
        out_shape=jax.ShapeDtypeStruct((R + MOE_BLK, D), F32),
        compiler_params=_params("arbitrary", "arbitrary"),
    )(block_e, n_used, block_valid, x_sorted, row_w, w1, w3, w2)


def _moe_routed(h, idx, wts, w1, w3, w2):
    T, D = h.shape
    E = w1.shape[0]
    A = T * TOP_K
    nb = -(-A // MOE_BLK) + E
    i32 = jnp.int32
    flat_t = jnp.repeat(jnp.arange(T, dtype=i32), TOP_K)
    _, st, sw = lax.sort((idx.reshape(-1), flat_t, wts.reshape(-1)), num_keys=1, is_stable=True)
    e_iota = jnp.arange(E, dtype=i32)
    onehot = idx[:, :, None] == e_iota[None, None, :]
    csum = jnp.cumsum(onehot.any(axis=1).astype(i32), axis=0)
    counts = csum[-1]
    pcounts = (counts + MOE_BLK - 1) // MOE_BLK * MOE_BLK
    start = jnp.cumsum(counts) - counts
    pend = jnp.cumsum(pcounts)
    pstart = pend - pcounts
    n_used = (pend[-1:] // MOE_BLK).astype(i32)
    blk0 = jnp.arange(nb, dtype=i32) * MOE_BLK
    block_e = jnp.minimum(jnp.searchsorted(pend, blk0, side='right'), E - 1).astype(i32)
    block_e = block_e[jnp.minimum(jnp.arange(nb), n_used[0] - 1)]
    off_in_e = blk0 - pstart[block_e]
    block_valid = jnp.clip(counts[block_e] - off_in_e, 0, MOE_BLK).astype(i32)
    c0 = jnp.clip(start[block_e] + off_in_e, 0, A)
    st_pad = jnp.pad(st, (0, MOE_BLK))
    sw_pad = jnp.pad(sw, (0, MOE_BLK))
    take_run = jax.vmap(lambda a, c: lax.dynamic_slice(a, (c,), (MOE_BLK,)), in_axes=(None, 0))
    in_run = jnp.arange(MOE_BLK, dtype=i32)[None, :] < block_valid[:, None]
    row_tok = jnp.where(in_run, take_run(st_pad, c0), 0).reshape(-1)
    row_w = jnp.where(in_run, take_run(sw_pad, c0), 0.0).reshape(-1, 1)
    x_sorted = jnp.take(h, row_tok, axis=0)
    y_sorted = _grouped_ffn(x_sorted, row_w, block_e, n_used, block_valid, w1, w3, w2)
    slot = (pstart[None, :] + csum - 1)[:, None, :]
    dest = jnp.sum(jnp.where(onehot, slot, 0), axis=-1)
    return jnp.sum(jnp.take(y_sorted, dest, axis=0), axis=1)


def _rope_tables(pos):
    half = ROT_DIM // 2
    inv = jnp.power(ROPE_THETA, -jnp.arange(half, dtype=F32) / half)
    ang = pos.astype(F32)[:, None] * inv[None, :]
    cos, sin = jnp.cos(ang), jnp.sin(ang)
    n = pos.shape[0]
    ones = jnp.ones((n, HEAD_DIM - ROT_DIM), F32)
    zeros = jnp.zeros((n, HEAD_DIM - half), F32)
    cos_t = jnp.concatenate([cos, cos, ones], axis=1)
    sin_a = jnp.concatenate([-sin, zeros], axis=1)
    sin_b = jnp.concatenate([jnp.zeros((n, half), F32), sin, jnp.zeros((n, HEAD_DIM - ROT_DIM), F32)], axis=1)
    return cos_t, sin_a, sin_b


def kernel(x_prompt, x_sample, c_prompt, c_sample, cache_k, cache_v, state_ssm, state_conv, page_table, w_ada, b_ada, norm_mix_pre, norm_mix_post, norm_ffn_pre, norm_ffn_post, w_in, conv_w, a_log, dt_bias, gdn_norm, w_out, w_router, e_bias, w1, w3, w2, ws1, ws3, ws2):
    B, S, D = x_prompt.shape
    Bs = x_sample.shape[0]
    assert x_sample.shape[1] == 1 and w_ada.shape[0] == 1
    assert Bs == ROW_BLK and S % MOBA_BLOCK == 0 and S % ROW_BLK == 0
    Tp = B * S
    T = Tp + Bs
    rb_per_seq = S // ROW_BLK
    past_len = page_table.shape[1] * PAGE_SIZE
    H = GDN_HEADS

    x_all = jnp.concatenate([x_prompt.reshape(Tp, D), x_sample.reshape(Bs, D)], axis=0)

    c_all = jnp.concatenate([c_prompt, c_sample], axis=0)
    n_c = c_all.shape[0]
    c_pad = jnp.pad(c_all, ((0, (-n_c) % 8), (0, 0)))
    mod = _matmul(c_pad, w_ada[0], b_ada, name="adaln", silu_in=True, tm=c_pad.shape[0])
    mod_rows = jnp.concatenate([jnp.repeat(mod[:B], ROW_BLK, axis=0), mod[B:B + Bs]], axis=0)

    h = _prenorm(x_all, norm_mix_pre, mod_rows, 0, 1, rb_per_seq)
    w_in0 = w_in[0]
    proj = _matmul(h, w_in0, name="in_proj", n_cols=OFF_AB)
    w_ab = jnp.pad(w_in0[:, OFF_AB:], ((0, 0), (0, HEAD_DIM - 2 * H)))
    ab = _matmul(h, w_ab, name="in_proj_gates")

    pos = jnp.concatenate([jnp.tile(jnp.arange(S, dtype=jnp.int32), B),
                           jnp.full((Bs,), past_len, jnp.int32)])
    qk_rot = _rope(proj, *_rope_tables(pos))
    k_rot = qk_rot[:, Q_ATTN:]
    v_new = proj[:, OFF_V:OFF_V + KV_ATTN]
    o_a_p = _attn_prefill(qk_rot, proj, B, S)
    grp = ATTN_HEADS // KV_HEADS
    q_s = qk_rot[Tp:, :Q_ATTN].reshape(Bs, ATTN_HEADS, HEAD_DIM)
    kx = jnp.repeat(k_rot[Tp:].reshape(Bs, KV_HEADS, HEAD_DIM), grp, axis=1)
    vx = jnp.repeat(v_new[Tp:].reshape(Bs, KV_HEADS, HEAD_DIM), grp, axis=1)
    o_a_s = _attn_decode(q_s, kx, vx, cache_k, cache_v, page_table)
    o_a = jnp.concatenate([o_a_p, o_a_s.reshape(Bs, Q_ATTN)], axis=0)

    alog_pad = jnp.pad(a_log.astype(F32), ((0, 0), (0, HEAD_DIM - H)))
    dtb_pad = jnp.pad(dt_bias.astype(F32), ((0, 0), (0, HEAD_DIM - H)))
    gb = _gates(ab, alog_pad, dtb_pad)
    qkv_p = _conv_prompt(proj, conv_w[0], B, S)
    buf_t = state_conv[0].transpose(1, 0, 2)
    qkv_s = _conv_sample(proj, buf_t, conv_w[0], Tp // Bs)
    n_chunks = S // GDN_CHUNK

    def chunk_rows(a):
        return a.reshape(B, n_chunks, GDN_CHUNK, H).transpose(0, 1, 3, 2)

    o_g_p, ssm_p = _gdn_prompt(qkv_p, chunk_rows(gb[:Tp, :H]), chunk_rows(gb[:Tp, H:2 * H]), B, S)
    o_g_s, ssm_s = _gdn_sample(qkv_s.reshape(Bs, 3 * H, HEAD_DIM), gb[Tp:].reshape(Bs, 1, HEAD_DIM),
                               state_ssm[0].astype(F32))
    o_g = jnp.concatenate([o_g_p, o_g_s.reshape(Bs, V_GDN)], axis=0)
    o_gn = _gated_norm(o_g, proj, gdn_norm)

    mix = _matmul(jnp.concatenate([o_a, o_gn], axis=1), w_out[0], name="out_proj")

    x1, h2, logits = _mid(x_all, mix, norm_mix_post, norm_ffn_pre, mod_rows, w_router[0], rb_per_seq)
    idx_l, wts_l = _route(logits, e_bias)
    routed = _moe_routed(h2, idx_l[:, :TOP_K], wts_l[:, :TOP_K], w1[0], w3[0], w2[0])
    shared = _shared_ffn(h2, ws1[0].astype(BF16), ws3[0].astype(BF16), ws2[0].astype(BF16))
    y_all = _final(x1, routed, shared, norm_ffn_post, mod_rows, rb_per_seq)

    y_prompt = y_all[:Tp].reshape(B, S, D)
    y_sample = y_all[Tp:].reshape(Bs, 1, D)
    k_prompt = k_rot[:Tp].reshape(1, B, S, KV_HEADS, HEAD_DIM)
    v_prompt = v_new[:Tp].reshape(1, B, S, KV_HEADS, HEAD_DIM)
    k_sample = k_rot[Tp:].reshape(1, Bs, 1, KV_HEADS, HEAD_DIM)
    v_sample = v_new[Tp:].reshape(1, Bs, 1, KV_HEADS, HEAD_DIM)
    pre_conv = proj[:, OFF_CONV:OFF_CONV + GDN_CONV_CH]
    conv_prompt = pre_conv[:Tp].reshape(B, S, GDN_CONV_CH)[:, S - (CONV_W - 1):][None]
    conv_sample = jnp.concatenate([state_conv[0][:, 1:], pre_conv[Tp:][:, None, :]], axis=1)[None]
    return (y_prompt, y_sample, k_prompt, v_prompt, k_sample, v_sample,
            ssm_p[None].astype(state_ssm.dtype), ssm_s[None].astype(state_ssm.dtype),
            conv_prompt.astype(state_conv.dtype), conv_sample.astype(state_conv.dtype))
```

```python
import functools

import jax
import jax.numpy as jnp
from jax import lax
from jax.experimental import pallas as pl
from jax.experimental.pallas import tpu as pltpu

F32 = jnp.float32
BF16 = jnp.bfloat16
HIGHEST = lax.Precision.HIGHEST

HEAD_DIM = 128
ATTN_HEADS = 16
KV_HEADS = 4
GDN_HEADS = 16
CONV_W = 4
GDN_CHUNK = 64
MOBA_BLOCK = 256
MOBA_TOPK = 3
ROPE_THETA = 500000.0
ROT_DIM = HEAD_DIM // 4
PAGE_SIZE = 128
N_GROUPS = 8
TOPK_GROUPS = 4
TOP_K = 8
ROUTED_SCALE = 2.5
NORM_EPS = 1e-6

Q_ATTN = ATTN_HEADS * HEAD_DIM
KV_ATTN = KV_HEADS * HEAD_DIM
QK_GDN = GDN_HEADS * HEAD_DIM
V_GDN = GDN_HEADS * HEAD_DIM
GDN_CONV_CH = 2 * QK_GDN + V_GDN
OFF_K = Q_ATTN
OFF_V = Q_ATTN + KV_ATTN
OFF_CONV = Q_ATTN + 2 * KV_ATTN
OFF_Z = OFF_CONV + GDN_CONV_CH
OFF_AB = OFF_Z + V_GDN

ROW_BLK = 128
MOE_BLK = 512
MOE_SUB = 256
NEG = -1e30
VMEM_LIMIT = 56 << 20


def _params(*sem):
    return pltpu.CompilerParams(dimension_semantics=sem, vmem_limit_bytes=VMEM_LIMIT)


def _pick(n, prefs):
    for p in prefs:
        if n % p == 0:
            return p
    return n


def _nt(a, b, **kw):
    return lax.dot_general(a, b, (((1,), (1,)), ((), ())), preferred_element_type=F32, **kw)


def _silu(x):
    return x * jax.nn.sigmoid(x)


def _rms(x, g):
    return x * lax.rsqrt(jnp.mean(x * x, axis=-1, keepdims=True) + NORM_EPS) * g


def _beats(a, b, a_first):
    return jnp.where(a > b, 1.0, jnp.where(a == b, jnp.where(a_first, 1.0, 0.0), 0.0))


def _mm_body(x_ref, w_ref, b_ref, o_ref, wbf_ref, *, silu_in):
    @pl.when(pl.program_id(1) == 0)
    def _cast_weight_tile():
        wbf_ref[...] = w_ref[...].astype(BF16)

    x = x_ref[...]
    if silu_in:
        x = _silu(x.astype(F32))
    acc = jnp.dot(x.astype(BF16), wbf_ref[...], preferred_element_type=F32)
    o_ref[...] = (acc + b_ref[...]).astype(o_ref.dtype)


def _matmul(x, w, b=None, *, name, n_cols=None, silu_in=False, out_dtype=F32, tm=None, tn=None):
    M, K = x.shape
    N = n_cols or w.shape[1]
    tm = tm or _pick(M, (640, 512, 256, 128))
    tn = tn or _pick(N, (512, 256, 128))
    if b is None:
        b = jnp.zeros((1, N), F32)
    return pl.pallas_call(
        functools.partial(_mm_body, silu_in=silu_in),
        name=name,
        grid=(N // tn, M // tm),
        in_specs=[
            pl.BlockSpec((tm, K), lambda n, m: (m, 0)),
            pl.BlockSpec((K, tn), lambda n, m: (0, n)),
            pl.BlockSpec((1, tn), lambda n, m: (0, n)),
        ],
        out_specs=pl.BlockSpec((tm, tn), lambda n, m: (m, n)),
        out_shape=jax.ShapeDtypeStruct((M, N), out_dtype),
        scratch_shapes=[pltpu.VMEM((K, tn), BF16)],
        compiler_params=_params("arbitrary", "arbitrary"),
    )(x, w, b)


def _prenorm_body(x_ref, g_ref, sh_ref, sc_ref, o_ref):
    y = _rms(x_ref[...], g_ref[...]) * (1.0 + sc_ref[...]) + sh_ref[...]
    o_ref[...] = y.astype(o_ref.dtype)


def _mod_spec(D, slot, rb_per_seq):
    return pl.BlockSpec((ROW_BLK, D), lambda r: (r // rb_per_seq, slot))


def _row_spec(D):
    return pl.BlockSpec((ROW_BLK, D), lambda r: (r, 0))


def _vec_spec(D):
    return pl.BlockSpec((1, D), lambda r: (0, 0))


def _prenorm(x, g, mod, slot_sh, slot_sc, rb_per_seq):
    T, D = x.shape
    return pl.pallas_call(
        _prenorm_body,
        name="prenorm",
        grid=(T // ROW_BLK,),
        in_specs=[_row_spec(D), _vec_spec(D), _mod_spec(D, slot_sh, rb_per_seq),
                  _mod_spec(D, slot_sc, rb_per_seq)],
        out_specs=_row_spec(D),
        out_shape=jax.ShapeDtypeStruct((T, D), BF16),
        compiler_params=_params("arbitrary"),
    )(x, g, mod, mod)


def _mid_body(x_ref, mix_ref, gpost_ref, gt_ref, gpre_ref, sh_ref, sc_ref, wr_ref,
              x1_ref, h_ref, hp_ref, lg_ref):
    x1 = x_ref[...] + gt_ref[...] * _rms(mix_ref[...], gpost_ref[...])
    x1_ref[...] = x1
    h = _rms(x1, gpre_ref[...]) * (1.0 + sc_ref[...]) + sh_ref[...]
    hb = h.astype(BF16)
    h_ref[...] = hb
    bits = pltpu.bitcast(hb.astype(F32), jnp.uint32)
    dh = bits.shape[1] // 2
    hp_ref[...] = jnp.bitwise_or(jnp.bitwise_and(bits[:, dh:], jnp.uint32(0xFFFF0000)),
                                 jnp.right_shift(bits[:, :dh], jnp.uint32(16)))
    lg_ref[...] = jnp.dot(h, wr_ref[...], precision=HIGHEST, preferred_element_type=F32)


def _mid(x, mix, g_post, g_pre, mod, w_router, rb_per_seq):
    T, D = x.shape
    E = w_router.shape[1]
    return pl.pallas_call(
        _mid_body,
        name="mid_norms_router",
        grid=(T // ROW_BLK,),
        in_specs=[_row_spec(D), _row_spec(D), _vec_spec(D), _mod_spec(D, 2, rb_per_seq),
                  _vec_spec(D), _mod_spec(D, 3, rb_per_seq), _mod_spec(D, 4, rb_per_seq),
                  pl.BlockSpec((D, E), lambda r: (0, 0))],
        out_specs=[_row_spec(D), _row_spec(D), _row_spec(D // 2), _row_spec(E)],
        out_shape=[jax.ShapeDtypeStruct((T, D), F32), jax.ShapeDtypeStruct((T, D), BF16),
                   jax.ShapeDtypeStruct((T, D // 2), jnp.uint32), jax.ShapeDtypeStruct((T, E), F32)],
        compiler_params=_params("arbitrary"),
    )(x, mix, g_post, mod, g_pre, mod, mod, w_router)


def _final_body(x1_ref, a_ref, b_ref, g_ref, gt_ref, o_ref):
    f = a_ref[...] + b_ref[...]
    o_ref[...] = x1_ref[...] + gt_ref[...] * _rms(f, g_ref[...])


def _final(x1, routed, shared, g_post, mod, rb_per_seq):
    T, D = x1.shape
    return pl.pallas_call(
        _final_body,
        name="final_norm",
        grid=(T // ROW_BLK,),
        in_specs=[_row_spec(D), _row_spec(D), _row_spec(D), _vec_spec(D),
                  _mod_spec(D, 5, rb_per_seq)],
        out_specs=_row_spec(D),
        out_shape=jax.ShapeDtypeStruct((T, D), F32),
        compiler_params=_params("arbitrary"),
    )(x1, routed, shared, g_post, mod)


def _rope_body(x_ref, c_ref, sa_ref, sb_ref, o_ref):
    x = x_ref[...]
    half = ROT_DIM // 2
    o_ref[...] = (x * c_ref[...] + pltpu.roll(x, HEAD_DIM - half, 1) * sa_ref[...]
                  + pltpu.roll(x, half, 1) * sb_ref[...])


def _rope(proj, cos_t, sin_a, sin_b):
    T = proj.shape[0]
    n_heads = ATTN_HEADS + KV_HEADS
    tr = _pick(T, (640, 512, 256, 128))
    tab = pl.BlockSpec((tr, HEAD_DIM), lambda r, h: (r, 0))
    blk = pl.BlockSpec((tr, HEAD_DIM), lambda r, h: (r, h))
    return pl.pallas_call(
        _rope_body,
        name="rope",
        grid=(T // tr, n_heads),
        in_specs=[blk, tab, tab, tab],
        out_specs=blk,
        out_shape=jax.ShapeDtypeStruct((T, n_heads * HEAD_DIM), F32),
        compiler_params=_params("arbitrary", "arbitrary"),
    )(proj, cos_t, sin_a, sin_b)


def _topk_mask(gate, lane, n_valid, n_cand):
    gate = jnp.where(lane < n_valid, gate, -jnp.inf)
    rank = jnp.zeros(gate.shape, F32)
    for m in range(n_cand):
        rank = rank + _beats(gate[:, m:m + 1], gate, m < lane)
    return jnp.where(lane < n_valid, jnp.where(rank < MOBA_TOPK, 1.0, 0.0), 0.0)


def _attn_prefill_body(q_ref, k_ref, v_ref, o_ref, m_ref, l_ref, acc_ref, sel_ref, *, nblk, grp):
    i = pl.program_id(2)
    blk = MOBA_BLOCK
    rows = grp * blk
    scale = HEAD_DIM ** -0.5
    qf = jnp.concatenate([q_ref[:, h * HEAD_DIM:(h + 1) * HEAD_DIM] for h in range(grp)], axis=0)
    qb = qf.astype(BF16)

    krow = lax.broadcasted_iota(jnp.int32, (HEAD_DIM, HEAD_DIM), 0)
    kmean = jnp.zeros((HEAD_DIM, HEAD_DIM), F32)
    for n in range(nblk):
        kmean = jnp.where(krow == n, jnp.mean(k_ref[n * blk:(n + 1) * blk, :], axis=0, keepdims=True), kmean)
    gate = _nt(qf, kmean, precision=HIGHEST)
    lane = lax.broadcasted_iota(jnp.int32, (rows, HEAD_DIM), 1)
    sel_ref[...] = _topk_mask(gate, lane, i, nblk)

    start = pl.multiple_of(i * blk, blk)
    s = _nt(qb, k_ref[pl.ds(start, blk), :].astype(BF16)) * scale
    qpos = jnp.bitwise_and(lax.broadcasted_iota(jnp.int32, (rows, blk), 0), blk - 1)
    kpos = lax.broadcasted_iota(jnp.int32, (rows, blk), 1)
    s = jnp.where(kpos <= qpos, s, NEG)
    m0 = jnp.max(s, axis=1, keepdims=True)
    p = jnp.exp(s - m0)
    m_ref[...] = m0
    l_ref[...] = jnp.sum(p, axis=1, keepdims=True)
    acc_ref[...] = jnp.dot(p.astype(BF16), v_ref[pl.ds(start, blk), :].astype(BF16),
                           preferred_element_type=F32)

    for n in range(nblk - 1):
        @pl.when(n < i)
        def _past_block(n=n):
            s = _nt(qb, k_ref[n * blk:(n + 1) * blk, :].astype(BF16)) * scale
            s = jnp.where(sel_ref[:, n:n + 1] > 0.0, s, NEG)
            m_old = m_ref[...]
            m_new = jnp.maximum(m_old, jnp.max(s, axis=1, keepdims=True))
            alpha = jnp.exp(m_old - m_new)
            p = jnp.exp(s - m_new)
            l_ref[...] = alpha * l_ref[...] + jnp.sum(p, axis=1, keepdims=True)
            acc_ref[...] = alpha * acc_ref[...] + jnp.dot(
                p.astype(BF16), v_ref[n * blk:(n + 1) * blk, :].astype(BF16),
                preferred_element_type=F32)
            m_ref[...] = m_new

    out = acc_ref[...] / l_ref[...]
    for h in range(grp):
        o_ref[:, h * HEAD_DIM:(h + 1) * HEAD_DIM] = out[h * blk:(h + 1) * blk].astype(o_ref.dtype)


def _attn_prefill(qk_rot, proj, B, S):
    nblk = S // MOBA_BLOCK
    grp = ATTN_HEADS // KV_HEADS
    rows = grp * MOBA_BLOCK
    k_col0 = OFF_K // HEAD_DIM
    v_col0 = OFF_V // HEAD_DIM
    return pl.pallas_call(
        functools.partial(_attn_prefill_body, nblk=nblk, grp=grp),
        name="attn_prefill",
        grid=(B, KV_HEADS, nblk),
        in_specs=[
            pl.BlockSpec((MOBA_BLOCK, grp * HEAD_DIM), lambda b, g, i: (b * nblk + i, g)),
            pl.BlockSpec((S, HEAD_DIM), lambda b, g, i: (b, k_col0 + g)),
            pl.BlockSpec((S, HEAD_DIM), lambda b, g, i: (b, v_col0 + g)),
        ],
        out_specs=pl.BlockSpec((MOBA_BLOCK, grp * HEAD_DIM), lambda b, g, i: (b * nblk + i, g)),
        out_shape=jax.ShapeDtypeStruct((B * S, Q_ATTN), BF16),
        scratch_shapes=[pltpu.VMEM((rows, 1), F32), pltpu.VMEM((rows, 1), F32),
                        pltpu.VMEM((rows, HEAD_DIM), F32), pltpu.VMEM((rows, HEAD_DIM), F32)],
        compiler_params=_params("arbitrary", "arbitrary", "arbitrary"),
    )(qk_rot, qk_rot, proj)


def _attn_decode_body(pt_ref, q_ref, kx_ref, vx_ref, *refs, n_pages):
    del pt_ref
    k_pages, v_pages, o_ref = refs[:n_pages], refs[n_pages:2 * n_pages], refs[2 * n_pages]
    H = ATTN_HEADS
    grp = ATTN_HEADS // KV_HEADS
    ppb = MOBA_BLOCK // PAGE_SIZE
    nblk = n_pages // ppb
    scale = HEAD_DIM ** -0.5
    q = q_ref[...]
    qb = q.astype(BF16)
    row = lax.broadcasted_iota(jnp.int32, (H, 1), 0)
    lane = lax.broadcasted_iota(jnp.int32, (H, HEAD_DIM), 1)
    s_own = jnp.sum(q * kx_ref[...], axis=1, keepdims=True) * scale
    vx = vx_ref[...]
    out = jnp.zeros((H, HEAD_DIM), F32)
    for h in range(KV_HEADS):
        kp = [k_pages[j][:, h, :] for j in range(n_pages)]
        gate = jnp.zeros((H, HEAD_DIM), F32)
        for n in range(nblk):
            ksum = kp[n * ppb].sum(axis=0, keepdims=True)
            for j in range(1, ppb):
                ksum = ksum + kp[n * ppb + j].sum(axis=0, keepdims=True)
            g_n = jnp.sum(q * (ksum / MOBA_BLOCK), axis=1, keepdims=True)
            gate = jnp.where(lane == n, g_n, gate)
        sel = _topk_mask(gate, lane, nblk, nblk)
        kf = jnp.concatenate(kp, axis=0).astype(BF16)
        s = _nt(qb, kf) * scale
        keymask = jnp.concatenate(
            [jnp.broadcast_to(sel[:, n:n + 1], (H, MOBA_BLOCK)) for n in range(nblk)], axis=1)
        s = jnp.where(keymask > 0.0, s, NEG)
        m = jnp.maximum(jnp.max(s, axis=1, keepdims=True), s_own)
        p = jnp.exp(s - m)
        e_own = jnp.exp(s_own - m)
        denom = jnp.sum(p, axis=1, keepdims=True) + e_own
        vf = jnp.concatenate([v_pages[j][:, h, :] for j in range(n_pages)], axis=0).astype(BF16)
        o_h = (jnp.dot(p.astype(BF16), vf, preferred_element_type=F32) + e_own * vx) / denom
        mine = jnp.logical_and(row >= h * grp, row < (h + 1) * grp)
        out = jnp.where(mine, o_h, out)
    o_ref[...] = out.astype(o_ref.dtype)


def _attn_decode(q_s, kx, vx, cache_k, cache_v, page_table):
    Bs, n_pages = page_table.shape
    assert n_pages % (MOBA_BLOCK // PAGE_SIZE) == 0
    vec = pl.BlockSpec((None, ATTN_HEADS, HEAD_DIM), lambda b, pt: (b, 0, 0))

    def page(j):
        return pl.BlockSpec((None, None, PAGE_SIZE, KV_HEADS, HEAD_DIM),
                            lambda b, pt: (0, pt[b, j], 0, 0, 0))

    pages = [page(j) for j in range(n_pages)]
    return pl.pallas_call(
        functools.partial(_attn_decode_body, n_pages=n_pages),
        name="attn_decode",
        grid_spec=pltpu.PrefetchScalarGridSpec(
            num_scalar_prefetch=1,
            grid=(Bs,),
            in_specs=[vec, vec, vec] + pages + pages,
            out_specs=vec,
        ),
        out_shape=jax.ShapeDtypeStruct((Bs, ATTN_HEADS, HEAD_DIM), BF16),
        compiler_params=_params("arbitrary"),
    )(page_table, q_s, kx, vx, *([cache_k] * n_pages), *([cache_v] * n_pages))


def _conv_finish(y, c):
    y = _silu(y)
    nrm = y * lax.rsqrt(jnp.sum(y * y, axis=1, keepdims=True) + 1e-6)
    return jnp.where(c < GDN_HEADS, nrm * (HEAD_DIM ** -0.5), jnp.where(c < 2 * GDN_HEADS, nrm, y))


def _conv_prompt_body(x_ref, w_ref, o_ref):
    x = x_ref[...]
    row = lax.broadcasted_iota(jnp.int32, x.shape, 0)
    y = x * w_ref[CONV_W - 1:CONV_W, :]
    for k in range(1, CONV_W):
        y = y + jnp.where(row >= k, pltpu.roll(x, k, 0), 0.0) * w_ref[CONV_W - 1 - k:CONV_W - k, :]
    o_ref[...] = _conv_finish(y, pl.program_id(1))


def _conv_prompt(proj, conv_w, B, S):
    n_tiles = GDN_CONV_CH // HEAD_DIM
    col0 = OFF_CONV // HEAD_DIM
    return pl.pallas_call(
        _conv_prompt_body,
        name="conv_prompt",
        grid=(B, n_tiles),
        in_specs=[pl.BlockSpec((S, HEAD_DIM), lambda b, c: (b, col0 + c)),
                  pl.BlockSpec((CONV_W, HEAD_DIM), lambda b, c: (0, c))],
        out_specs=pl.BlockSpec((S, HEAD_DIM), lambda b, c: (b, c)),
        out_shape=jax.ShapeDtypeStruct((B * S, GDN_CONV_CH), F32),
        compiler_params=_params("arbitrary", "arbitrary"),
    )(proj, conv_w)


def _conv_sample_body(x_ref, buf_ref, w_ref, o_ref):
    y = x_ref[...] * w_ref[CONV_W - 1:CONV_W, :]
    for i in range(CONV_W - 1):
        y = y + buf_ref[i] * w_ref[i:i + 1, :]
    o_ref[...] = _conv_finish(y, pl.program_id(0))


def _conv_sample(proj, buf_t, conv_w, row_blk0):
    Bs = buf_t.shape[1]
    n_tiles = GDN_CONV_CH // HEAD_DIM
    col0 = OFF_CONV // HEAD_DIM
    return pl.pallas_call(
        _conv_sample_body,
        name="conv_sample",
        grid=(n_tiles,),
        in_specs=[pl.BlockSpec((Bs, HEAD_DIM), lambda c: (row_blk0, col0 + c)),
                  pl.BlockSpec((CONV_W - 1, Bs, HEAD_DIM), lambda c: (0, 0, c)),
                  pl.BlockSpec((CONV_W, HEAD_DIM), lambda c: (0, c))],
        out_specs=pl.BlockSpec((Bs, HEAD_DIM), lambda c: (0, c)),
        out_shape=jax.ShapeDtypeStruct((Bs, GDN_CONV_CH), F32),
        compiler_params=_params("arbitrary"),
    )(proj, buf_t, conv_w)


def _gates_body(ab_ref, alog_ref, dtb_ref, o_ref):
    x = ab_ref[...]
    sp_in = x + dtb_ref[...]
    softplus = jnp.maximum(sp_in, 0.0) + jnp.log1p(jnp.exp(-jnp.abs(sp_in)))
    g = -jnp.exp(alog_ref[...]) * softplus
    lane = lax.broadcasted_iota(jnp.int32, x.shape, 1)
    o_ref[...] = jnp.where(lane < GDN_HEADS, g, jax.nn.sigmoid(x))


def _gates(ab, alog_pad, dtb_pad):
    T = ab.shape[0]
    return pl.pallas_call(
        _gates_body,
        name="gdn_gates",
        grid=(T // ROW_BLK,),
        in_specs=[_row_spec(HEAD_DIM), _vec_spec(HEAD_DIM), _vec_spec(HEAD_DIM)],
        out_specs=_row_spec(HEAD_DIM),
        out_shape=jax.ShapeDtypeStruct((T, HEAD_DIM), F32),
        compiler_params=_params("arbitrary"),
    )(ab, alog_pad, dtb_pad)


def _mmb(a, b):
    return jnp.dot(a.astype(BF16), b.astype(BF16), preferred_element_type=F32)


def _split_bf16(a):
    hi = a.astype(BF16)
    return hi, (a - hi.astype(F32)).astype(BF16)


def _mm3(a, b):
    ah, al = _split_bf16(a)
    bh, bl = _split_bf16(b)

    def d(x, y):
        return jnp.dot(x, y, preferred_element_type=F32)

    return d(ah, bh) + (d(ah, bl) + d(al, bh))


def _unit_lower_inverse(lows):
    C = lows[0].shape[0]
    ii = lax.broadcasted_iota(jnp.int32, (C, C), 0)
    jj = lax.broadcasted_iota(jnp.int32, (C, C), 1)
    eye = jnp.where(ii == jj, 1.0, 0.0)
    invs = [eye - low for low in lows]
    pws = [_mm3(low, low) for low in lows]
    span = 2
    while span < C:
        invs = [inv + _mm3(inv, pw) for inv, pw in zip(invs, pws)]
        span *= 2
        if span < C:
            pws = [_mm3(pw, pw) for pw in pws]
    return invs


GDN_HEAD_GROUP = 8


def _gdn_chunk_body(q_ref, k_ref, v_ref, g_ref, b_ref, o_ref, s_out_ref, s_ref):
    n = pl.program_id(1)
    C = GDN_CHUNK

    @pl.when(n == 0)
    def _zero_state():
        s_ref[...] = jnp.zeros_like(s_ref)

    ii = lax.broadcasted_iota(jnp.int32, (C, C), 0)
    jj = lax.broadcasted_iota(jnp.int32, (C, C), 1)
    eye = ii == jj
    tril = jj <= ii
    for h0 in range(0, GDN_HEADS, GDN_HEAD_GROUP):
        hs = range(h0, h0 + GDN_HEAD_GROUP)
        cols = [slice(h * HEAD_DIM, (h + 1) * HEAD_DIM) for h in hs]
        q = [q_ref[:, c] for c in cols]
        k = [k_ref[:, c] for c in cols]
        v = [v_ref[:, c] for c in cols]
        gc_col = [jnp.sum(jnp.where(tril, g_ref[h:h + 1, :], 0.0), axis=1, keepdims=True) for h in hs]
        gc_row = [jnp.sum(jnp.where(eye, x, 0.0), axis=0, keepdims=True) for x in gc_col]
        beta = [jnp.sum(jnp.where(eye, b_ref[h:h + 1, :], 0.0), axis=1, keepdims=True) for h in hs]
        decay = [jnp.exp(jnp.where(tril, c_ - r_, -jnp.inf)) for c_, r_ in zip(gc_col, gc_row)]
        g_last = [x[C - 1:C, :] for x in gc_col]
        e_gc = [jnp.exp(x) for x in gc_col]
        kbeta = [k_ * b_ for k_, b_ in zip(k, beta)]
        k16 = [x.astype(BF16) for x in k]
        low = [jnp.where(ii > jj, _nt(kb.astype(BF16), kk) * d, 0.0) for kb, kk, d in zip(kbeta, k16, decay)]
        tmat = [t.astype(BF16) for t in _unit_lower_inverse(low)]
        u = [_mmb(t, v_ * b_) for t, v_, b_ in zip(tmat, v, beta)]
        w = [_mmb(t, kb * e) for t, kb, e in zip(tmat, kbeta, e_gc)]
        attn = [_nt(q_.astype(BF16), kk) * d for q_, kk, d in zip(q, k16, decay)]
        state = [s_ref[h] for h in hs]
        v_new = [u_ - _mmb(w_, s_) for u_, w_, s_ in zip(u, w, state)]
        for c, q_, e, s_, a_, vn in zip(cols, q, e_gc, state, attn, v_new):
            o_ref[:, c] = _mmb(q_ * e, s_) + _mmb(a_, vn)
        for h, k_, gl, gc, s_, vn in zip(hs, k, g_last, gc_col, state, v_new):
            k_tail = k_ * jnp.exp(gl - gc)
            s_ref[h] = s_ * jnp.exp(gl) + _mmb(k_tail.T, vn)

    @pl.when(n == pl.num_programs(1) - 1)
    def _emit_state():
        s_out_ref[...] = s_ref[...]


def _gdn_prompt(qkv, g_rows, b_rows, B, S):
    N = S // GDN_CHUNK
    W = GDN_HEADS * HEAD_DIM

    def part(j):
        return pl.BlockSpec((GDN_CHUNK, W), lambda b, n: (b * N + n, j))

    gspec = pl.BlockSpec((None, None, GDN_HEADS, GDN_CHUNK), lambda b, n: (b, n, 0, 0))
    return pl.pallas_call(
        _gdn_chunk_body,
        name="gdn_prompt",
        grid=(B, N),
        in_specs=[part(0), part(1), part(2), gspec, gspec],
        out_specs=[pl.BlockSpec((GDN_CHUNK, W), lambda b, n: (b * N + n, 0)),
                   pl.BlockSpec((None, GDN_HEADS, HEAD_DIM, HEAD_DIM), lambda b, n: (b, 0, 0, 0))],
        out_shape=[jax.ShapeDtypeStruct((B * S, W), F32),
                   jax.ShapeDtypeStruct((B, GDN_HEADS, HEAD_DIM, HEAD_DIM), F32)],
        scratch_shapes=[pltpu.VMEM((GDN_HEADS, HEAD_DIM, HEAD_DIM), F32)],
        compiler_params=_params("arbitrary", "arbitrary"),
    )(qkv, qkv, qkv, g_rows, b_rows)


def _gdn_step_body(qkv_ref, gb_ref, s_ref, o_ref, s_out_ref):
    H = GDN_HEADS
    ii = lax.broadcasted_iota(jnp.int32, (HEAD_DIM, HEAD_DIM), 0)
    jj = lax.broadcasted_iota(jnp.int32, (HEAD_DIM, HEAD_DIM), 1)
    eye = ii == jj
    lane = lax.broadcasted_iota(jnp.int32, (1, HEAD_DIM), 1)
    gb = gb_ref[...]
    for h in range(H):
        q_col = jnp.sum(jnp.where(eye, qkv_ref[h:h + 1, :], 0.0), axis=1, keepdims=True)
        k_col = jnp.sum(jnp.where(eye, qkv_ref[H + h:H + h + 1, :], 0.0), axis=1, keepdims=True)
        v_row = qkv_ref[2 * H + h:2 * H + h + 1, :]
        g = jnp.sum(jnp.where(lane == h, gb, 0.0), axis=1, keepdims=True)
        beta = jnp.sum(jnp.where(lane == H + h, gb, 0.0), axis=1, keepdims=True)
        state = s_ref[h] * jnp.exp(g)
        kv = jnp.sum(k_col * state, axis=0, keepdims=True)
        delta = (v_row - kv) * beta
        state = state + k_col * delta
        s_out_ref[h] = state
        o_ref[h:h + 1, :] = jnp.sum(q_col * state, axis=0, keepdims=True)


def _gdn_sample(qkv3, gb3, state):
    Bs = qkv3.shape[0]
    H = GDN_HEADS
    sspec = pl.BlockSpec((None, H, HEAD_DIM, HEAD_DIM), lambda b: (b, 0, 0, 0))
    return pl.pallas_call(
        _gdn_step_body,
        name="gdn_sample",
        grid=(Bs,),
        in_specs=[pl.BlockSpec((None, 3 * H, HEAD_DIM), lambda b: (b, 0, 0)),
                  pl.BlockSpec((None, 1, HEAD_DIM), lambda b: (b, 0, 0)), sspec],
        out_specs=[pl.BlockSpec((None, H, HEAD_DIM), lambda b: (b, 0, 0)), sspec],
        out_shape=[jax.ShapeDtypeStruct((Bs, H, HEAD_DIM), F32),
                   jax.ShapeDtypeStruct(state.shape, F32)],
        compiler_params=_params("arbitrary"),
    )(qkv3, gb3, state)


def _gated_norm_body(o_ref, z_ref, g_ref, out_ref):
    for h in range(o_ref.shape[1] // HEAD_DIM):
        c = slice(h * HEAD_DIM, (h + 1) * HEAD_DIM)
        out_ref[:, c] = (_rms(o_ref[:, c], g_ref[...]) * _silu(z_ref[:, c])).astype(out_ref.dtype)


def _gated_norm(o_g, proj, gdn_norm):
    T = o_g.shape[0]
    W = 8 * HEAD_DIM
    assert OFF_Z % W == 0 and V_GDN % W == 0
    tm = _pick(T, (640, 512, 256, 128))
    z_col0 = OFF_Z // W
    return pl.pallas_call(
        _gated_norm_body,
        name="gdn_gated_norm",
        grid=(T // tm, V_GDN // W),
        in_specs=[pl.BlockSpec((tm, W), lambda r, c: (r, c)),
                  pl.BlockSpec((tm, W), lambda r, c: (r, z_col0 + c)),
                  pl.BlockSpec((1, HEAD_DIM), lambda r, c: (0, 0))],
        out_specs=pl.BlockSpec((tm, W), lambda r, c: (r, c)),
        out_shape=jax.ShapeDtypeStruct((T, V_GDN), BF16),
        compiler_params=_params("arbitrary", "arbitrary"),
    )(o_g, proj, gdn_norm)


def _route_body(lg_ref, bias_ref, idx_ref, wt_ref):
    scores = jax.nn.sigmoid(lg_ref[...])
    biased = scores + bias_ref[...]
    shape = scores.shape
    E = shape[1]
    per_grp = E // N_GROUPS
    lane_i = lax.broadcasted_iota(jnp.int32, shape, 1)
    lane = lane_i.astype(F32)
    ninf = -jnp.inf

    def in_grp(g):
        return jnp.logical_and(lane_i >= g * per_grp, lane_i < (g + 1) * per_grp)

    gscore = []
    for g in range(N_GROUPS):
        xg = jnp.where(in_grp(g), biased, ninf)
        m1 = jnp.max(xg, axis=1, keepdims=True)
        i1 = jnp.min(jnp.where(xg == m1, lane, float(E)), axis=1, keepdims=True)
        m2 = jnp.max(jnp.where(lane == i1, ninf, xg), axis=1, keepdims=True)
        gscore.append(m1 + m2)
    keep = jnp.zeros(shape, F32)
    for g in range(N_GROUPS):
        rank = jnp.zeros_like(gscore[g])
        for o in range(N_GROUPS):
            if o != g:
                rank = rank + _beats(gscore[o], gscore[g], o < g)
        keep = jnp.where(in_grp(g), jnp.where(rank < TOPK_GROUPS, 1.0, 0.0), keep)

    cur = jnp.where(keep > 0.0, biased, ninf)
    idx = jnp.zeros(shape, F32)
    wts = jnp.zeros(shape, F32)
    wsum = jnp.zeros((shape[0], 1), F32)
    for k in range(TOP_K):
        mk = jnp.max(cur, axis=1, keepdims=True)
        ik = jnp.min(jnp.where(cur == mk, lane, float(E)), axis=1, keepdims=True)
        hit = lane == ik
        wk = jnp.sum(jnp.where(hit, scores, 0.0), axis=1, keepdims=True)
        cur = jnp.where(hit, ninf, cur)
        idx = jnp.where(lane_i == k, ik, idx)
        wts = jnp.where(lane_i == k, wk, wts)
        wsum = wsum + wk
    idx_ref[...] = idx.astype(jnp.int32)
    wt_ref[...] = wts / wsum * ROUTED_SCALE


def _route(logits, e_bias):
    T, E = logits.shape
    return pl.pallas_call(
        _route_body,
        name="route",
        grid=(T // ROW_BLK,),
        in_specs=[_row_spec(E), _vec_spec(E)],
        out_specs=[_row_spec(E), _row_spec(E)],
        out_shape=[jax.ShapeDtypeStruct((T, E), jnp.int32), jax.ShapeDtypeStruct((T, E), F32)],
        compiler_params=_params("arbitrary"),
    )(logits, e_bias)


def _ffn_body(x_ref, w1_ref, w3_ref, w2_ref, o_ref):
    x = x_ref[...]
    a = _silu(jnp.dot(x, w1_ref[...], preferred_element_type=F32))
    a = a * jnp.dot(x, w3_ref[...], preferred_element_type=F32)
    o_ref[...] = jnp.dot(a.astype(BF16), w2_ref[...], preferred_element_type=F32)


def _shared_ffn(x, w1, w3, w2):
    T, D = x.shape
    F = w1.shape[1]
    tm = _pick(T, (640, 512, 256, 128))
    return pl.pallas_call(
        _ffn_body,
        name="shared_ffn",
        grid=(T // tm,),
        in_specs=[pl.BlockSpec((tm, D), lambda r: (r, 0)),
                  pl.BlockSpec((D, F), lambda r: (0, 0)), pl.BlockSpec((D, F), lambda r: (0, 0)),
                  pl.BlockSpec((F, D), lambda r: (0, 0))],
        out_specs=pl.BlockSpec((tm, D), lambda r: (r, 0)),
        out_shape=jax.ShapeDtypeStruct((T, D), F32),
        compiler_params=_params("arbitrary"),
    )(x, w1, w3, w2)


def _grouped_ffn_body(be_ref, nu_ref, bv_ref, c0_ref, ta_ref, tb_ref, na_ref, nb_ref, hp_hbm,
                      w1_ref, w3_ref, w2_ref, o_ref, xbuf, sem, h1_ref, h3_ref, act_ref):
    del be_ref
    i = pl.program_id(0)
    p = pl.program_id(1)
    n_used = nu_ref[0]
    used = i < n_used
    n_valid = bv_ref[i]
    slot = lax.rem(i, 2)
    subs = [(s0, min(MOE_SUB, MOE_BLK - s0)) for s0 in range(0, MOE_BLK, MOE_SUB)]

    def row_copy(tok, slot_, r):
        return pltpu.make_async_copy(hp_hbm.at[pl.ds(tok, 1), :], xbuf.at[slot_, pl.ds(r, 1), :], sem.at[slot_])

    def request(blk, slot_, ca_ref, cb_ref, r_lo, r_hi):
        off = lax.rem(c0_ref[blk], MOE_BLK)

        def body(r, carry):
            j = off + r
            tok = jnp.where(j < MOE_BLK, ca_ref[0, jnp.minimum(j, MOE_BLK - 1)],
                            cb_ref[0, jnp.maximum(j - MOE_BLK, 0)])
            row_copy(tok, slot_, r).start()
            return carry

        lax.fori_loop(r_lo, r_hi, body, 0)

    @pl.when(jnp.logical_and(i == 0, p == 0))
    def _prime():
        xbuf[...] = jnp.zeros_like(xbuf)
        request(0, 0, ta_ref, tb_ref, 0, n_valid)

    nxt = i + 1

    @pl.when(nxt < n_used)
    def _prefetch():
        nv = bv_ref[jnp.minimum(nxt, bv_ref.shape[0] - 1)]
        q = MOE_BLK // 4
        request(nxt, 1 - slot, na_ref, nb_ref, jnp.minimum(p * q, nv), jnp.minimum((p + 1) * q, nv))

    @pl.when(jnp.logical_and(used, p == 0))
    def _await_rows():
        def body(r, carry):
            row_copy(0, slot, 0).wait()
            return carry

        lax.fori_loop(0, n_valid, body, 0)

    @pl.when(jnp.logical_and(used, p < 2))
    def _up():
        w1 = w1_ref[...].astype(BF16)
        w3 = w3_ref[...].astype(BF16)
        for s0, sz in subs:
            @pl.when(s0 < n_valid)
            def _sub(s0=s0, sz=sz):
                xu = xbuf[slot, s0:s0 + sz, :]
                xbits = jnp.where(p == 0, jnp.left_shift(xu, jnp.uint32(16)),
                                  jnp.bitwise_and(xu, jnp.uint32(0xFFFF0000)))
                x = pltpu.bitcast(xbits, F32).astype(BF16)
                a1 = jnp.dot(x, w1, preferred_element_type=F32)
                a3 = jnp.dot(x, w3, preferred_element_type=F32)

                @pl.when(p == 0)
                def _first():
                    h1_ref[s0:s0 + sz, :] = a1
                    h3_ref[s0:s0 + sz, :] = a3

                @pl.when(p == 1)
                def _second():
                    act_ref[s0:s0 + sz, :] = (_silu(h1_ref[s0:s0 + sz, :] + a1)
                                              * (h3_ref[s0:s0 + sz, :] + a3)).astype(BF16)

    @pl.when(jnp.logical_and(used, p >= 2))
    def _down():
        w2 = w2_ref[...].astype(BF16)
        for s0, sz in subs:
            @pl.when(s0 < n_valid)
            def _sub(s0=s0, sz=sz):
                o_ref[s0:s0 + sz, :] = jnp.dot(act_ref[s0:s0 + sz, :], w2, preferred_element_type=F32)

            @pl.when(s0 >= n_valid)
            def _pad(s0=s0, sz=sz):
                o_ref[s0:s0 + sz, :] = jnp.zeros((sz, o_ref.shape[1]), F32)

    @pl.when(jnp.logical_not(used))
    def _unused_block():
        o_ref[...] = jnp.zeros_like(o_ref)


def _grouped_ffn(h_packed, tok_chunks, c0, block_e, n_used, block_valid, w1, w3, w2):
    T, Dh = h_packed.shape
    D = 2 * Dh
    E, _, F = w1.shape
    nb = block_e.shape[0]
    n_chunks = tok_chunks.shape[0]

    def last_used(i, nu):
        return jnp.minimum(i, nu[0] - 1)

    def k_half(i, p, nu):
        return jnp.where(i < nu[0], jnp.minimum(p, 1), 1)

    def n_half(i, p, nu):
        return jnp.where(i < nu[0], jnp.maximum(p - 2, 0), 1)

    def chunk(blk_of, plus):
        def index(i, p, be, nu, bv, c0_):
            q = c0_[last_used(blk_of(i), nu)] // MOE_BLK + plus
            return (jnp.minimum(q, n_chunks - 1), 0, 0)
        return pl.BlockSpec((None, 1, MOE_BLK), index, memory_space=pltpu.SMEM)

    return pl.pallas_call(
        _grouped_ffn_body,
        name="moe_grouped_ffn",
        grid_spec=pltpu.PrefetchScalarGridSpec(
            num_scalar_prefetch=4,
            grid=(nb, 4),
            in_specs=[
                chunk(lambda i: i, 0), chunk(lambda i: i, 1),
                chunk(lambda i: i + 1, 0), chunk(lambda i: i + 1, 1),
                pl.BlockSpec(memory_space=pl.ANY),
                pl.BlockSpec((None, Dh, F), lambda i, p, be, nu, bv, c0_: (be[i], k_half(i, p, nu), 0)),
                pl.BlockSpec((None, Dh, F), lambda i, p, be, nu, bv, c0_: (be[i], k_half(i, p, nu), 0)),
                pl.BlockSpec((None, F, Dh), lambda i, p, be, nu, bv, c0_: (be[i], 0, n_half(i, p, nu))),
            ],
            out_specs=pl.BlockSpec((MOE_BLK, Dh),
                                   lambda i, p, be, nu, bv, c0_: (jnp.minimum(i, nu[0]),
                                                                  jnp.where(i < nu[0], jnp.maximum(p - 2, 0), 0))),
            scratch_shapes=[pltpu.VMEM((2, MOE_BLK, Dh), jnp.uint32), pltpu.SemaphoreType.DMA((2,)),
                            pltpu.VMEM((MOE_BLK, F), F32), pltpu.VMEM((MOE_BLK, F), F32),
                            pltpu.VMEM((MOE_BLK, F), BF16)],
        ),
        out_shape=jax.ShapeDtypeStruct(((nb + 1) * MOE_BLK, D), F32),
        compiler_params=_params("arbitrary", "arbitrary"),
    )(block_e, n_used, block_valid, c0, tok_chunks, tok_chunks, tok_chunks, tok_chunks, h_packed, w1, w3, w2)


def _moe_routed(h_packed, idx, wts, w1, w3, w2):
    T = h_packed.shape[0]
    E = w1.shape[0]
    A = T * TOP_K
    nb = -(-A // MOE_BLK) + E
    i32 = jnp.int32
    flat_t = jnp.repeat(jnp.arange(T, dtype=i32), TOP_K)
    _, st = lax.sort((idx.reshape(-1), flat_t), num_keys=1, is_stable=True)
    e_iota = jnp.arange(E, dtype=i32)
    onehot = idx[:, :, None] == e_iota[None, None, :]
    csum = jnp.cumsum(onehot.any(axis=1).astype(i32), axis=0)
    counts = csum[-1]
    pcounts = (counts + MOE_BLK - 1) // MOE_BLK * MOE_BLK
    start = jnp.cumsum(counts) - counts
    pend = jnp.cumsum(pcounts)
    pstart = pend - pcounts
    n_used = (pend[-1:] // MOE_BLK).astype(i32)
    blk0 = jnp.arange(nb, dtype=i32) * MOE_BLK
    block_e = jnp.minimum(jnp.searchsorted(pend, blk0, side='right'), E - 1).astype(i32)
    block_e = block_e[jnp.minimum(jnp.arange(nb), n_used[0] - 1)]
    off_in_e = blk0 - pstart[block_e]
    block_valid = jnp.clip(counts[block_e] - off_in_e, 0, MOE_BLK).astype(i32)
    c0 = jnp.clip(start[block_e] + off_in_e, 0, A).astype(i32)
    n_chunks = -(-A // MOE_BLK) + 1
    tok_chunks = jnp.pad(st, (0, n_chunks * MOE_BLK - A)).reshape(n_chunks, 1, MOE_BLK)
    y_sorted = _grouped_ffn(h_packed, tok_chunks, c0, block_e, n_used, block_valid, w1, w3, w2)
    slot = (pstart[None, :] + csum - 1)[:, None, :]
    dest = jnp.sum(jnp.where(onehot, slot, 0), axis=-1)
    y_tok = jnp.take(y_sorted, dest, axis=0, mode="clip")
    return jnp.sum(y_tok * wts[:, :, None], axis=1)


def _rope_tables(pos):
    half = ROT_DIM // 2
    inv = jnp.power(ROPE_THETA, -jnp.arange(half, dtype=F32) / half)
    ang = pos.astype(F32)[:, None] * inv[None, :]
    cos, sin = jnp.cos(ang), jnp.sin(ang)
    n = pos.shape[0]
    ones = jnp.ones((n, HEAD_DIM - ROT_DIM), F32)
    zeros = jnp.zeros((n, HEAD_DIM - half), F32)
    cos_t = jnp.concatenate([cos, cos, ones], axis=1)
    sin_a = jnp.concatenate([-sin, zeros], axis=1)
    sin_b = jnp.concatenate([jnp.zeros((n, half), F32), sin, jnp.zeros((n, HEAD_DIM - ROT_DIM), F32)], axis=1)
    return cos_t, sin_a, sin_b


def kernel(x_prompt, x_sample, c_prompt, c_sample, cache_k, cache_v, state_ssm, state_conv, page_table, w_ada, b_ada, norm_mix_pre, norm_mix_post, norm_ffn_pre, norm_ffn_post, w_in, conv_w, a_log, dt_bias, gdn_norm, w_out, w_router, e_bias, w1, w3, w2, ws1, ws3, ws2):
    B, S, D = x_prompt.shape
    Bs = x_sample.shape[0]
    assert x_sample.shape[1] == 1 and w_ada.shape[0] == 1
    assert Bs == ROW_BLK and S % MOBA_BLOCK == 0 and S % ROW_BLK == 0
    Tp = B * S
    T = Tp + Bs
    rb_per_seq = S // ROW_BLK
    past_len = page_table.shape[1] * PAGE_SIZE
    H = GDN_HEADS

    x_all = jnp.concatenate([x_prompt.reshape(Tp, D), x_sample.reshape(Bs, D)], axis=0)

    c_all = jnp.concatenate([c_prompt, c_sample], axis=0)
    n_c = c_all.shape[0]
    c_pad = jnp.pad(c_all, ((0, (-n_c) % 8), (0, 0)))
    mod = _matmul(c_pad, w_ada[0], b_ada, name="adaln", silu_in=True, tm=c_pad.shape[0])
    mod_rows = jnp.concatenate([jnp.repeat(mod[:B], ROW_BLK, axis=0), mod[B:B + Bs]], axis=0)

    h = _prenorm(x_all, norm_mix_pre, mod_rows, 0, 1, rb_per_seq)
    w_in0 = w_in[0]
    proj = _matmul(h, w_in0, name="in_proj", n_cols=OFF_AB)
    w_ab = jnp.pad(w_in0[:, OFF_AB:], ((0, 0), (0, HEAD_DIM - 2 * H)))
    ab = _matmul(h, w_ab, name="in_proj_gates")

    pos = jnp.concatenate([jnp.tile(jnp.arange(S, dtype=jnp.int32), B),
                           jnp.full((Bs,), past_len, jnp.int32)])
    qk_rot = _rope(proj, *_rope_tables(pos))
    k_rot = qk_rot[:, Q_ATTN:]
    v_new = proj[:, OFF_V:OFF_V + KV_ATTN]
    o_a_p = _attn_prefill(qk_rot, proj, B, S)
    grp = ATTN_HEADS // KV_HEADS
    q_s = qk_rot[Tp:, :Q_ATTN].reshape(Bs, ATTN_HEADS, HEAD_DIM)
    kx = jnp.repeat(k_rot[Tp:].reshape(Bs, KV_HEADS, HEAD_DIM), grp, axis=1)
    vx = jnp.repeat(v_new[Tp:].reshape(Bs, KV_HEADS, HEAD_DIM), grp, axis=1)
    o_a_s = _attn_decode(q_s, kx, vx, cache_k, cache_v, page_table)
    o_a = jnp.concatenate([o_a_p, o_a_s.reshape(Bs, Q_ATTN)], axis=0)

    alog_pad = jnp.pad(a_log.astype(F32), ((0, 0), (0, HEAD_DIM - H)))
    dtb_pad = jnp.pad(dt_bias.astype(F32), ((0, 0), (0, HEAD_DIM - H)))
    gb = _gates(ab, alog_pad, dtb_pad)
    qkv_p = _conv_prompt(proj, conv_w[0], B, S)
    buf_t = state_conv[0].transpose(1, 0, 2)
    qkv_s = _conv_sample(proj, buf_t, conv_w[0], Tp // Bs)
    n_chunks = S // GDN_CHUNK

    def chunk_rows(a):
        return a.reshape(B, n_chunks, GDN_CHUNK, H).transpose(0, 1, 3, 2)

    o_g_p, ssm_p = _gdn_prompt(qkv_p, chunk_rows(gb[:Tp, :H]), chunk_rows(gb[:Tp, H:2 * H]), B, S)
    o_g_s, ssm_s = _gdn_sample(qkv_s.reshape(Bs, 3 * H, HEAD_DIM), gb[Tp:].reshape(Bs, 1, HEAD_DIM),
                               state_ssm[0].astype(F32))
    o_g = jnp.concatenate([o_g_p, o_g_s.reshape(Bs, V_GDN)], axis=0)
    o_gn = _gated_norm(o_g, proj, gdn_norm)

    mix = _matmul(jnp.concatenate([o_a, o_gn], axis=1), w_out[0], name="out_proj")

    x1, h2, h2_packed, logits = _mid(x_all, mix, norm_mix_post, norm_ffn_pre, mod_rows, w_router[0], rb_per_seq)
    idx_l, wts_l = _route(logits, e_bias)
    routed = _moe_routed(h2_packed, idx_l[:, :TOP_K], wts_l[:, :TOP_K], w1[0], w3[0], w2[0])
    shared = _shared_ffn(h2, ws1[0].astype(BF16), ws3[0].astype(BF16), ws2[0].astype(BF16))
    y_all = _final(x1, routed, shared, norm_ffn_post, mod_rows, rb_per_seq)

    y_prompt = y_all[:Tp].reshape(B, S, D)
    y_sample = y_all[Tp:].reshape(Bs, 1, D)
    k_prompt = k_rot[:Tp].reshape(1, B, S, KV_HEADS, HEAD_DIM)
    v_prompt = v_new[:Tp].reshape(1, B, S, KV_HEADS, HEAD_DIM)
    k_sample = k_rot[Tp:].reshape(1, Bs, 1, KV_HEADS, HEAD_DIM)
    v_sample = v_new[Tp:].reshape(1, Bs, 1, KV_HEADS, HEAD_DIM)
    def pre_conv_rows(r0, r1):
        return lax.slice(proj, (r0, OFF_CONV), (r1, OFF_CONV + GDN_CONV_CH))

    conv_prompt = jnp.stack([pre_conv_rows((b + 1) * S - (CONV_W - 1), (b + 1) * S) for b in range(B)])[None]
    conv_sample = jnp.concatenate([state_conv[0][:, 1:], pre_conv_rows(Tp, T)[:, None, :]], axis=1)[None]
    return (y_prompt, y_sample, k_prompt, v_prompt, k_sample, v_sample,
            ssm_p[None].astype(state_ssm.dtype), ssm_s[None].astype(state_ssm.dtype),
            conv_prompt.astype(state_conv.dtype), conv_sample.astype(state_conv.dtype))
```

```python
import functools

import jax
import jax.numpy as jnp
from jax import lax
from jax.experimental import pallas as pl
from jax.experimental.pallas import tpu as pltpu

F32 = jnp.float32
BF16 = jnp.bfloat16
HIGHEST = lax.Precision.HIGHEST

HEAD_DIM = 128
ATTN_HEADS = 16
KV_HEADS = 4
GDN_HEADS = 16
CONV_W = 4
GDN_CHUNK = 64
MOBA_BLOCK = 256
MOBA_TOPK = 3
ROPE_THETA = 500000.0
ROT_DIM = HEAD_DIM // 4
PAGE_SIZE = 128
N_GROUPS = 8
TOPK_GROUPS = 4
TOP_K = 8
ROUTED_SCALE = 2.5
NORM_EPS = 1e-6

Q_ATTN = ATTN_HEADS * HEAD_DIM
KV_ATTN = KV_HEADS * HEAD_DIM
QK_GDN = GDN_HEADS * HEAD_DIM
V_GDN = GDN_HEADS * HEAD_DIM
GDN_CONV_CH = 2 * QK_GDN + V_GDN
OFF_K = Q_ATTN
OFF_V = Q_ATTN + KV_ATTN
OFF_CONV = Q_ATTN + 2 * KV_ATTN
OFF_Z = OFF_CONV + GDN_CONV_CH
OFF_AB = OFF_Z + V_GDN

ROW_BLK = 128
MOE_BLK = 576
MOE_SUB = 288
TOK_CHUNK = 1024
ROW_UNROLL = 8
NEG = -1e30
VMEM_LIMIT = 56 << 20


def _params(*sem):
    return pltpu.CompilerParams(dimension_semantics=sem, vmem_limit_bytes=VMEM_LIMIT)


def _pick(n, prefs):
    for p in prefs:
        if n % p == 0:
            return p
    return n


def _nt(a, b, **kw):
    return lax.dot_general(a, b, (((1,), (1,)), ((), ())), preferred_element_type=F32, **kw)


def _silu(x):
    return x * jax.nn.sigmoid(x)


def _rms(x, g):
    return x * lax.rsqrt(jnp.mean(x * x, axis=-1, keepdims=True) + NORM_EPS) * g


def _beats(a, b, a_first):
    return jnp.where(a > b, 1.0, jnp.where(a == b, jnp.where(a_first, 1.0, 0.0), 0.0))


def _mm_body(x_ref, w_ref, b_ref, o_ref, wbf_ref, *, silu_in):
    @pl.when(pl.program_id(1) == 0)
    def _cast_weight_tile():
        wbf_ref[...] = w_ref[...].astype(BF16)

    x = x_ref[...]
    if silu_in:
        x = _silu(x.astype(F32))
    acc = jnp.dot(x.astype(BF16), wbf_ref[...], preferred_element_type=F32)
    o_ref[...] = (acc + b_ref[...]).astype(o_ref.dtype)


def _matmul(x, w, b=None, *, name, n_cols=None, silu_in=False, out_dtype=F32, tm=None, tn=None):
    M, K = x.shape
    N = n_cols or w.shape[1]
    tm = tm or _pick(M, (640, 512, 256, 128))
    tn = tn or _pick(N, (512, 256, 128))
    if b is None:
        b = jnp.zeros((1, N), F32)
    return pl.pallas_call(
        functools.partial(_mm_body, silu_in=silu_in),
        name=name,
        grid=(N // tn, M // tm),
        in_specs=[
            pl.BlockSpec((tm, K), lambda n, m: (m, 0)),
            pl.BlockSpec((K, tn), lambda n, m: (0, n)),
            pl.BlockSpec((1, tn), lambda n, m: (0, n)),
        ],
        out_specs=pl.BlockSpec((tm, tn), lambda n, m: (m, n)),
        out_shape=jax.ShapeDtypeStruct((M, N), out_dtype),
        scratch_shapes=[pltpu.VMEM((K, tn), BF16)],
        compiler_params=_params("arbitrary", "arbitrary"),
    )(x, w, b)


def _prenorm_body(x_ref, g_ref, sh_ref, sc_ref, o_ref):
    y = _rms(x_ref[...], g_ref[...]) * (1.0 + sc_ref[...]) + sh_ref[...]
    o_ref[...] = y.astype(o_ref.dtype)


def _mod_spec(D, slot, rb_per_seq):
    return pl.BlockSpec((ROW_BLK, D), lambda r: (r // rb_per_seq, slot))


def _row_spec(D):
    return pl.BlockSpec((ROW_BLK, D), lambda r: (r, 0))


def _vec_spec(D):
    return pl.BlockSpec((1, D), lambda r: (0, 0))


def _prenorm(x, g, mod, slot_sh, slot_sc, rb_per_seq):
    T, D = x.shape
    return pl.pallas_call(
        _prenorm_body,
        name="prenorm",
        grid=(T // ROW_BLK,),
        in_specs=[_row_spec(D), _vec_spec(D), _mod_spec(D, slot_sh, rb_per_seq),
                  _mod_spec(D, slot_sc, rb_per_seq)],
        out_specs=_row_spec(D),
        out_shape=jax.ShapeDtypeStruct((T, D), BF16),
        compiler_params=_params("arbitrary"),
    )(x, g, mod, mod)


def _mid_body(x_ref, mix_ref, gpost_ref, gt_ref, gpre_ref, sh_ref, sc_ref, wr_ref,
              x1_ref, h_ref, hp_ref, lg_ref):
    x1 = x_ref[...] + gt_ref[...] * _rms(mix_ref[...], gpost_ref[...])
    x1_ref[...] = x1
    h = _rms(x1, gpre_ref[...]) * (1.0 + sc_ref[...]) + sh_ref[...]
    hb = h.astype(BF16)
    h_ref[...] = hb
    bits = pltpu.bitcast(hb.astype(F32), jnp.uint32)
    dh = bits.shape[1] // 2
    hp_ref[...] = jnp.bitwise_or(jnp.bitwise_and(bits[:, dh:], jnp.uint32(0xFFFF0000)),
                                 jnp.right_shift(bits[:, :dh], jnp.uint32(16)))
    lg_ref[...] = jnp.dot(h, wr_ref[...], precision=HIGHEST, preferred_element_type=F32)


def _mid(x, mix, g_post, g_pre, mod, w_router, rb_per_seq):
    T, D = x.shape
    E = w_router.shape[1]
    return pl.pallas_call(
        _mid_body,
        name="mid_norms_router",
        grid=(T // ROW_BLK,),
        in_specs=[_row_spec(D), _row_spec(D), _vec_spec(D), _mod_spec(D, 2, rb_per_seq),
                  _vec_spec(D), _mod_spec(D, 3, rb_per_seq), _mod_spec(D, 4, rb_per_seq),
                  pl.BlockSpec((D, E), lambda r: (0, 0))],
        out_specs=[_row_spec(D), _row_spec(D), _row_spec(D // 2), _row_spec(E)],
        out_shape=[jax.ShapeDtypeStruct((T, D), F32), jax.ShapeDtypeStruct((T, D), BF16),
                   jax.ShapeDtypeStruct((T, D // 2), jnp.uint32), jax.ShapeDtypeStruct((T, E), F32)],
        compiler_params=_params("arbitrary"),
    )(x, mix, g_post, mod, g_pre, mod, mod, w_router)


def _final_body(x1_ref, a_ref, b_ref, g_ref, gt_ref, o_ref):
    f = a_ref[...] + b_ref[...]
    o_ref[...] = x1_ref[...] + gt_ref[...] * _rms(f, g_ref[...])


def _final(x1, routed, shared, g_post, mod, rb_per_seq):
    T, D = x1.shape
    return pl.pallas_call(
        _final_body,
        name="final_norm",
        grid=(T // ROW_BLK,),
        in_specs=[_row_spec(D), _row_spec(D), _row_spec(D), _vec_spec(D),
                  _mod_spec(D, 5, rb_per_seq)],
        out_specs=_row_spec(D),
        out_shape=jax.ShapeDtypeStruct((T, D), F32),
        compiler_params=_params("arbitrary"),
    )(x1, routed, shared, g_post, mod)


def _rope_body(x_ref, c_ref, sa_ref, sb_ref, o_ref):
    x = x_ref[...]
    half = ROT_DIM // 2
    o_ref[...] = (x * c_ref[...] + pltpu.roll(x, HEAD_DIM - half, 1) * sa_ref[...]
                  + pltpu.roll(x, half, 1) * sb_ref[...])


def _rope(proj, cos_t, sin_a, sin_b):
    T = proj.shape[0]
    n_heads = ATTN_HEADS + KV_HEADS
    tr = _pick(T, (640, 512, 256, 128))
    tab = pl.BlockSpec((tr, HEAD_DIM), lambda r, h: (r, 0))
    blk = pl.BlockSpec((tr, HEAD_DIM), lambda r, h: (r, h))
    return pl.pallas_call(
        _rope_body,
        name="rope",
        grid=(T // tr, n_heads),
        in_specs=[blk, tab, tab, tab],
        out_specs=blk,
        out_shape=jax.ShapeDtypeStruct((T, n_heads * HEAD_DIM), F32),
        compiler_params=_params("arbitrary", "arbitrary"),
    )(proj, cos_t, sin_a, sin_b)


def _topk_mask(gate, lane, n_valid, n_cand):
    gate = jnp.where(lane < n_valid, gate, -jnp.inf)
    rank = jnp.zeros(gate.shape, F32)
    for m in range(n_cand):
        rank = rank + _beats(gate[:, m:m + 1], gate, m < lane)
    return jnp.where(lane < n_valid, jnp.where(rank < MOBA_TOPK, 1.0, 0.0), 0.0)


def _attn_prefill_body(q_ref, k_ref, v_ref, o_ref, m_ref, l_ref, acc_ref, sel_ref, *, nblk, grp):
    i = pl.program_id(2)
    blk = MOBA_BLOCK
    rows = grp * blk
    scale = HEAD_DIM ** -0.5
    qf = jnp.concatenate([q_ref[:, h * HEAD_DIM:(h + 1) * HEAD_DIM] for h in range(grp)], axis=0)
    qb = qf.astype(BF16)

    krow = lax.broadcasted_iota(jnp.int32, (HEAD_DIM, HEAD_DIM), 0)
    kmean = jnp.zeros((HEAD_DIM, HEAD_DIM), F32)
    for n in range(nblk):
        kmean = jnp.where(krow == n, jnp.mean(k_ref[n * blk:(n + 1) * blk, :], axis=0, keepdims=True), kmean)
    gate = _nt(qf, kmean, precision=HIGHEST)
    lane = lax.broadcasted_iota(jnp.int32, (rows, HEAD_DIM), 1)
    sel_ref[...] = _topk_mask(gate, lane, i, nblk)

    start = pl.multiple_of(i * blk, blk)
    s = _nt(qb, k_ref[pl.ds(start, blk), :].astype(BF16)) * scale
    qpos = jnp.bitwise_and(lax.broadcasted_iota(jnp.int32, (rows, blk), 0), blk - 1)
    kpos = lax.broadcasted_iota(jnp.int32, (rows, blk), 1)
    s = jnp.where(kpos <= qpos, s, NEG)
    m0 = jnp.max(s, axis=1, keepdims=True)
    p = jnp.exp(s - m0)
    m_ref[...] = m0
    l_ref[...] = jnp.sum(p, axis=1, keepdims=True)
    acc_ref[...] = jnp.dot(p.astype(BF16), v_ref[pl.ds(start, blk), :].astype(BF16),
                           preferred_element_type=F32)

    for n in range(nblk - 1):
        @pl.when(n < i)
        def _past_block(n=n):
            s = _nt(qb, k_ref[n * blk:(n + 1) * blk, :].astype(BF16)) * scale
            s = jnp.where(sel_ref[:, n:n + 1] > 0.0, s, NEG)
            m_old = m_ref[...]
            m_new = jnp.maximum(m_old, jnp.max(s, axis=1, keepdims=True))
            alpha = jnp.exp(m_old - m_new)
            p = jnp.exp(s - m_new)
            l_ref[...] = alpha * l_ref[...] + jnp.sum(p, axis=1, keepdims=True)
            acc_ref[...] = alpha * acc_ref[...] + jnp.dot(
                p.astype(BF16), v_ref[n * blk:(n + 1) * blk, :].astype(BF16),
                preferred_element_type=F32)
            m_ref[...] = m_new

    out = acc_ref[...] / l_ref[...]
    for h in range(grp):
        o_ref[:, h * HEAD_DIM:(h + 1) * HEAD_DIM] = out[h * blk:(h + 1) * blk].astype(o_ref.dtype)


def _attn_prefill(qk_rot, proj, B, S):
    nblk = S // MOBA_BLOCK
    grp = ATTN_HEADS // KV_HEADS
    rows = grp * MOBA_BLOCK
    k_col0 = OFF_K // HEAD_DIM
    v_col0 = OFF_V // HEAD_DIM
    return pl.pallas_call(
        functools.partial(_attn_prefill_body, nblk=nblk, grp=grp),
        name="attn_prefill",
        grid=(B, KV_HEADS, nblk),
        in_specs=[
            pl.BlockSpec((MOBA_BLOCK, grp * HEAD_DIM), lambda b, g, i: (b * nblk + i, g)),
            pl.BlockSpec((S, HEAD_DIM), lambda b, g, i: (b, k_col0 + g)),
            pl.BlockSpec((S, HEAD_DIM), lambda b, g, i: (b, v_col0 + g)),
        ],
        out_specs=pl.BlockSpec((MOBA_BLOCK, grp * HEAD_DIM), lambda b, g, i: (b * nblk + i, g)),
        out_shape=jax.ShapeDtypeStruct((B * S, Q_ATTN), BF16),
        scratch_shapes=[pltpu.VMEM((rows, 1), F32), pltpu.VMEM((rows, 1), F32),
                        pltpu.VMEM((rows, HEAD_DIM), F32), pltpu.VMEM((rows, HEAD_DIM), F32)],
        compiler_params=_params("arbitrary", "arbitrary", "arbitrary"),
    )(qk_rot, qk_rot, proj)


def _attn_decode_body(pt_ref, q_ref, kx_ref, vx_ref, *refs, n_pages):
    del pt_ref
    k_pages, v_pages, o_ref = refs[:n_pages], refs[n_pages:2 * n_pages], refs[2 * n_pages]
    H = ATTN_HEADS
    grp = ATTN_HEADS // KV_HEADS
    ppb = MOBA_BLOCK // PAGE_SIZE
    nblk = n_pages // ppb
    page_rows = PAGE_SIZE * KV_HEADS
    blk_cols = MOBA_BLOCK * KV_HEADS
    ncol = n_pages * page_rows
    scale = HEAD_DIM ** -0.5
    q = q_ref[...]
    row1 = lax.broadcasted_iota(jnp.int32, (H, 1), 0)
    lane = lax.broadcasted_iota(jnp.int32, (H, HEAD_DIM), 1)

    gate = jnp.zeros((H, HEAD_DIM), F32)
    for n in range(nblk):
        ks8 = k_pages[n * ppb][...].reshape(page_rows // 8, 8, HEAD_DIM).sum(axis=0)
        for j in range(1, ppb):
            ks8 = ks8 + k_pages[n * ppb + j][...].reshape(page_rows // 8, 8, HEAD_DIM).sum(axis=0)
        ksum = ks8[:KV_HEADS] + ks8[KV_HEADS:]
        kexp = jnp.zeros((H, HEAD_DIM), F32)
        for h in range(KV_HEADS):
            mine_h = jnp.logical_and(row1 >= h * grp, row1 < (h + 1) * grp)
            kexp = jnp.where(mine_h, ksum[h:h + 1, :], kexp)
        g_n = jnp.sum(q * kexp, axis=1, keepdims=True) / MOBA_BLOCK
        gate = jnp.where(lane == n, g_n, gate)
    sel = _topk_mask(gate, lane, nblk, nblk)

    kall = jnp.concatenate([kp[...].astype(BF16) for kp in k_pages], axis=0)
    s = _nt(q.astype(BF16), kall) * scale
    col = lax.broadcasted_iota(jnp.int32, (H, ncol), 1)
    row = lax.broadcasted_iota(jnp.int32, (H, ncol), 0)
    own_head = jnp.bitwise_and(col, KV_HEADS - 1) * grp == jnp.bitwise_and(row, -grp)
    picked = jnp.concatenate(
        [jnp.broadcast_to(sel[:, n:n + 1], (H, blk_cols)) for n in range(nblk)], axis=1)
    s = jnp.where(jnp.logical_and(own_head, picked > 0.0), s, NEG)
    s_own = jnp.sum(q * kx_ref[...], axis=1, keepdims=True) * scale
    m = jnp.maximum(jnp.max(s, axis=1, keepdims=True), s_own)
    p = jnp.exp(s - m)
    e_own = jnp.exp(s_own - m)
    denom = jnp.sum(p, axis=1, keepdims=True) + e_own
    vall = jnp.concatenate([vp[...].astype(BF16) for vp in v_pages], axis=0)
    out = (jnp.dot(p.astype(BF16), vall, preferred_element_type=F32) + e_own * vx_ref[...]) / denom
    o_ref[...] = out.astype(o_ref.dtype)


def _attn_decode(q_s, kx, vx, cache_k, cache_v, page_table):
    Bs, n_pages = page_table.shape
    assert n_pages % (MOBA_BLOCK // PAGE_SIZE) == 0
    assert KV_HEADS & (KV_HEADS - 1) == 0 and (ATTN_HEADS // KV_HEADS) & (ATTN_HEADS // KV_HEADS - 1) == 0
    vec = pl.BlockSpec((None, ATTN_HEADS, HEAD_DIM), lambda b, pt: (b, 0, 0))

    def page(j):
        return pl.BlockSpec((PAGE_SIZE * KV_HEADS, HEAD_DIM), lambda b, pt: (pt[b, j], 0))

    pages = [page(j) for j in range(n_pages)]
    return pl.pallas_call(
        functools.partial(_attn_decode_body, n_pages=n_pages),
        name="attn_decode",
        grid_spec=pltpu.PrefetchScalarGridSpec(
            num_scalar_prefetch=1,
            grid=(Bs,),
            in_specs=[vec, vec, vec] + pages + pages,
            out_specs=vec,
        ),
        out_shape=jax.ShapeDtypeStruct((Bs, ATTN_HEADS, HEAD_DIM), BF16),
        compiler_params=_params("arbitrary"),
    )(page_table, q_s, kx, vx, *([cache_k] * n_pages), *([cache_v] * n_pages))


def _conv_finish(y, c):
    y = _silu(y)
    nrm = y * lax.rsqrt(jnp.sum(y * y, axis=1, keepdims=True) + 1e-6)
    return jnp.where(c < GDN_HEADS, nrm * (HEAD_DIM ** -0.5), jnp.where(c < 2 * GDN_HEADS, nrm, y))


def _conv_prompt_body(x_ref, w_ref, o_ref):
    x = x_ref[...]
    row = lax.broadcasted_iota(jnp.int32, x.shape, 0)
    y = x * w_ref[CONV_W - 1:CONV_W, :]
    for k in range(1, CONV_W):
        y = y + jnp.where(row >= k, pltpu.roll(x, k, 0), 0.0) * w_ref[CONV_W - 1 - k:CONV_W - k, :]
    o_ref[...] = _conv_finish(y, pl.program_id(1))


def _conv_prompt(proj, conv_w, B, S):
    n_tiles = GDN_CONV_CH // HEAD_DIM
    col0 = OFF_CONV // HEAD_DIM
    return pl.pallas_call(
        _conv_prompt_body,
        name="conv_prompt",
        grid=(B, n_tiles),
        in_specs=[pl.BlockSpec((S, HEAD_DIM), lambda b, c: (b, col0 + c)),
                  pl.BlockSpec((CONV_W, HEAD_DIM), lambda b, c: (0, c))],
        out_specs=pl.BlockSpec((S, HEAD_DIM), lambda b, c: (b, c)),
        out_shape=jax.ShapeDtypeStruct((B * S, GDN_CONV_CH), F32),
        compiler_params=_params("arbitrary", "arbitrary"),
    )(proj, conv_w)


def _conv_sample_body(x_ref, buf_ref, w_ref, o_ref):
    y = x_ref[...] * w_ref[CONV_W - 1:CONV_W, :]
    for i in range(CONV_W - 1):
        y = y + buf_ref[i] * w_ref[i:i + 1, :]
    o_ref[...] = _conv_finish(y, pl.program_id(0))


def _conv_sample(proj, buf_t, conv_w, row_blk0):
    Bs = buf_t.shape[1]
    n_tiles = GDN_CONV_CH // HEAD_DIM
    col0 = OFF_CONV // HEAD_DIM
    return pl.pallas_call(
        _conv_sample_body,
        name="conv_sample",
        grid=(n_tiles,),
        in_specs=[pl.BlockSpec((Bs, HEAD_DIM), lambda c: (row_blk0, col0 + c)),
                  pl.BlockSpec((CONV_W - 1, Bs, HEAD_DIM), lambda c: (0, 0, c)),
                  pl.BlockSpec((CONV_W, HEAD_DIM), lambda c: (0, c))],
        out_specs=pl.BlockSpec((Bs, HEAD_DIM), lambda c: (0, c)),
        out_shape=jax.ShapeDtypeStruct((Bs, GDN_CONV_CH), F32),
        compiler_params=_params("arbitrary"),
    )(proj, buf_t, conv_w)


def _gates_body(ab_ref, alog_ref, dtb_ref, o_ref):
    x = ab_ref[...]
    sp_in = x + dtb_ref[...]
    softplus = jnp.maximum(sp_in, 0.0) + jnp.log1p(jnp.exp(-jnp.abs(sp_in)))
    g = -jnp.exp(alog_ref[...]) * softplus
    lane = lax.broadcasted_iota(jnp.int32, x.shape, 1)
    o_ref[...] = jnp.where(lane < GDN_HEADS, g, jax.nn.sigmoid(x))


def _gates(ab, alog_pad, dtb_pad):
    T = ab.shape[0]
    return pl.pallas_call(
        _gates_body,
        name="gdn_gates",
        grid=(T // ROW_BLK,),
        in_specs=[_row_spec(HEAD_DIM), _vec_spec(HEAD_DIM), _vec_spec(HEAD_DIM)],
        out_specs=_row_spec(HEAD_DIM),
        out_shape=jax.ShapeDtypeStruct((T, HEAD_DIM), F32),
        compiler_params=_params("arbitrary"),
    )(ab, alog_pad, dtb_pad)


def _mmb(a, b):
    return jnp.dot(a.astype(BF16), b.astype(BF16), preferred_element_type=F32)


def _split_bf16(a):
    hi = a.astype(BF16)
    return hi, (a - hi.astype(F32)).astype(BF16)


def _mm3(a, b):
    ah, al = _split_bf16(a)
    bh, bl = _split_bf16(b)

    def d(x, y):
        return jnp.dot(x, y, preferred_element_type=F32)

    return d(ah, bh) + (d(ah, bl) + d(al, bh))


def _unit_lower_inverse(lows):
    C = lows[0].shape[0]
    ii = lax.broadcasted_iota(jnp.int32, (C, C), 0)
    jj = lax.broadcasted_iota(jnp.int32, (C, C), 1)
    eye = jnp.where(ii == jj, 1.0, 0.0)
    invs = [eye - low for low in lows]
    pws = [_mm3(low, low) for low in lows]
    span = 2
    while span < C:
        invs = [inv + _mm3(inv, pw) for inv, pw in zip(invs, pws)]
        span *= 2
        if span < C:
            pws = [_mm3(pw, pw) for pw in pws]
    return invs


GDN_HEAD_GROUP = 8


def _gdn_chunk_body(q_ref, k_ref, v_ref, g_ref, b_ref, o_ref, s_out_ref, s_ref):
    n = pl.program_id(1)
    C = GDN_CHUNK

    @pl.when(n == 0)
    def _zero_state():
        s_ref[...] = jnp.zeros_like(s_ref)

    ii = lax.broadcasted_iota(jnp.int32, (C, C), 0)
    jj = lax.broadcasted_iota(jnp.int32, (C, C), 1)
    eye = ii == jj
    tril = jj <= ii
    for h0 in range(0, GDN_HEADS, GDN_HEAD_GROUP):
        hs = range(h0, h0 + GDN_HEAD_GROUP)
        cols = [slice(h * HEAD_DIM, (h + 1) * HEAD_DIM) for h in hs]
        q = [q_ref[:, c] for c in cols]
        k = [k_ref[:, c] for c in cols]
        v = [v_ref[:, c] for c in cols]
        gc_col = [jnp.sum(jnp.where(tril, g_ref[h:h + 1, :], 0.0), axis=1, keepdims=True) for h in hs]
        gc_row = [jnp.sum(jnp.where(eye, x, 0.0), axis=0, keepdims=True) for x in gc_col]
        beta = [jnp.sum(jnp.where(eye, b_ref[h:h + 1, :], 0.0), axis=1, keepdims=True) for h in hs]
        decay = [jnp.exp(jnp.where(tril, c_ - r_, -jnp.inf)) for c_, r_ in zip(gc_col, gc_row)]
        g_last = [x[C - 1:C, :] for x in gc_col]
        e_gc = [jnp.exp(x) for x in gc_col]
        kbeta = [k_ * b_ for k_, b_ in zip(k, beta)]
        k16 = [x.astype(BF16) for x in k]
        low = [jnp.where(ii > jj, _nt(kb.astype(BF16), kk) * d, 0.0) for kb, kk, d in zip(kbeta, k16, decay)]
        tmat = [t.astype(BF16) for t in _unit_lower_inverse(low)]
        u = [_mmb(t, v_ * b_) for t, v_, b_ in zip(tmat, v, beta)]
        w = [_mmb(t, kb * e) for t, kb, e in zip(tmat, kbeta, e_gc)]
        attn = [_nt(q_.astype(BF16), kk) * d for q_, kk, d in zip(q, k16, decay)]
        state = [s_ref[h] for h in hs]
        v_new = [u_ - _mmb(w_, s_) for u_, w_, s_ in zip(u, w, state)]
        for c, q_, e, s_, a_, vn in zip(cols, q, e_gc, state, attn, v_new):
            o_ref[:, c] = _mmb(q_ * e, s_) + _mmb(a_, vn)
        for h, k_, gl, gc, s_, vn in zip(hs, k, g_last, gc_col, state, v_new):
            k_tail = k_ * jnp.exp(gl - gc)
            s_ref[h] = s_ * jnp.exp(gl) + _mmb(k_tail.T, vn)

    @pl.when(n == pl.num_programs(1) - 1)
    def _emit_state():
        s_out_ref[...] = s_ref[...]


def _gdn_prompt(qkv, g_rows, b_rows, B, S):
    N = S // GDN_CHUNK
    W = GDN_HEADS * HEAD_DIM

    def part(j):
        return pl.BlockSpec((GDN_CHUNK, W), lambda b, n: (b * N + n, j))

    gspec = pl.BlockSpec((None, None, GDN_HEADS, GDN_CHUNK), lambda b, n: (b, n, 0, 0))
    return pl.pallas_call(
        _gdn_chunk_body,
        name="gdn_prompt",
        grid=(B, N),
        in_specs=[part(0), part(1), part(2), gspec, gspec],
        out_specs=[pl.BlockSpec((GDN_CHUNK, W), lambda b, n: (b * N + n, 0)),
                   pl.BlockSpec((None, GDN_HEADS, HEAD_DIM, HEAD_DIM), lambda b, n: (b, 0, 0, 0))],
        out_shape=[jax.ShapeDtypeStruct((B * S, W), F32),
                   jax.ShapeDtypeStruct((B, GDN_HEADS, HEAD_DIM, HEAD_DIM), F32)],
        scratch_shapes=[pltpu.VMEM((GDN_HEADS, HEAD_DIM, HEAD_DIM), F32)],
        compiler_params=_params("arbitrary", "arbitrary"),
    )(qkv, qkv, qkv, g_rows, b_rows)


def _gdn_step_body(qkv_ref, gb_ref, s_ref, o_ref, s_out_ref):
    H = GDN_HEADS
    ii = lax.broadcasted_iota(jnp.int32, (HEAD_DIM, HEAD_DIM), 0)
    jj = lax.broadcasted_iota(jnp.int32, (HEAD_DIM, HEAD_DIM), 1)
    eye = ii == jj
    lane = lax.broadcasted_iota(jnp.int32, (1, HEAD_DIM), 1)
    gb = gb_ref[...]
    for h in range(H):
        q_col = jnp.sum(jnp.where(eye, qkv_ref[h:h + 1, :], 0.0), axis=1, keepdims=True)
        k_col = jnp.sum(jnp.where(eye, qkv_ref[H + h:H + h + 1, :], 0.0), axis=1, keepdims=True)
        v_row = qkv_ref[2 * H + h:2 * H + h + 1, :]
        g = jnp.sum(jnp.where(lane == h, gb, 0.0), axis=1, keepdims=True)
        beta = jnp.sum(jnp.where(lane == H + h, gb, 0.0), axis=1, keepdims=True)
        state = s_ref[h] * jnp.exp(g)
        kv = jnp.sum(k_col * state, axis=0, keepdims=True)
        delta = (v_row - kv) * beta
        state = state + k_col * delta
        s_out_ref[h] = state
        o_ref[h:h + 1, :] = jnp.sum(q_col * state, axis=0, keepdims=True)


def _gdn_sample(qkv3, gb3, state):
    Bs = qkv3.shape[0]
    H = GDN_HEADS
    sspec = pl.BlockSpec((None, H, HEAD_DIM, HEAD_DIM), lambda b: (b, 0, 0, 0))
    return pl.pallas_call(
        _gdn_step_body,
        name="gdn_sample",
        grid=(Bs,),
        in_specs=[pl.BlockSpec((None, 3 * H, HEAD_DIM), lambda b: (b, 0, 0)),
                  pl.BlockSpec((None, 1, HEAD_DIM), lambda b: (b, 0, 0)), sspec],
        out_specs=[pl.BlockSpec((None, H, HEAD_DIM), lambda b: (b, 0, 0)), sspec],
        out_shape=[jax.ShapeDtypeStruct((Bs, H, HEAD_DIM), F32),
                   jax.ShapeDtypeStruct(state.shape, F32)],
        compiler_params=_params("arbitrary"),
    )(qkv3, gb3, state)


def _gated_norm_body(o_ref, z_ref, g_ref, out_ref):
    for h in range(o_ref.shape[1] // HEAD_DIM):
        c = slice(h * HEAD_DIM, (h + 1) * HEAD_DIM)
        out_ref[:, c] = (_rms(o_ref[:, c], g_ref[...]) * _silu(z_ref[:, c])).astype(out_ref.dtype)


def _gated_norm(o_g, proj, gdn_norm):
    T = o_g.shape[0]
    W = 8 * HEAD_DIM
    assert OFF_Z % W == 0 and V_GDN % W == 0
    tm = _pick(T, (640, 512, 256, 128))
    z_col0 = OFF_Z // W
    return pl.pallas_call(
        _gated_norm_body,
        name="gdn_gated_norm",
        grid=(T // tm, V_GDN // W),
        in_specs=[pl.BlockSpec((tm, W), lambda r, c: (r, c)),
                  pl.BlockSpec((tm, W), lambda r, c: (r, z_col0 + c)),
                  pl.BlockSpec((1, HEAD_DIM), lambda r, c: (0, 0))],
        out_specs=pl.BlockSpec((tm, W), lambda r, c: (r, c)),
        out_shape=jax.ShapeDtypeStruct((T, V_GDN), BF16),
        compiler_params=_params("arbitrary", "arbitrary"),
    )(o_g, proj, gdn_norm)


def _route_body(lg_ref, bias_ref, idx_ref, wt_ref):
    scores = jax.nn.sigmoid(lg_ref[...])
    biased = scores + bias_ref[...]
    shape = scores.shape
    E = shape[1]
    per_grp = E // N_GROUPS
    lane_i = lax.broadcasted_iota(jnp.int32, shape, 1)
    lane = lane_i.astype(F32)
    ninf = -jnp.inf

    def in_grp(g):
        return jnp.logical_and(lane_i >= g * per_grp, lane_i < (g + 1) * per_grp)

    gscore = []
    for g in range(N_GROUPS):
        xg = jnp.where(in_grp(g), biased, ninf)
        m1 = jnp.max(xg, axis=1, keepdims=True)
        i1 = jnp.min(jnp.where(xg == m1, lane, float(E)), axis=1, keepdims=True)
        m2 = jnp.max(jnp.where(lane == i1, ninf, xg), axis=1, keepdims=True)
        gscore.append(m1 + m2)
    keep = jnp.zeros(shape, F32)
    for g in range(N_GROUPS):
        rank = jnp.zeros_like(gscore[g])
        for o in range(N_GROUPS):
            if o != g:
                rank = rank + _beats(gscore[o], gscore[g], o < g)
        keep = jnp.where(in_grp(g), jnp.where(rank < TOPK_GROUPS, 1.0, 0.0), keep)

    cur = jnp.where(keep > 0.0, biased, ninf)
    idx = jnp.zeros(shape, F32)
    wts = jnp.zeros(shape, F32)
    wsum = jnp.zeros((shape[0], 1), F32)
    for k in range(TOP_K):
        mk = jnp.max(cur, axis=1, keepdims=True)
        ik = jnp.min(jnp.where(cur == mk, lane, float(E)), axis=1, keepdims=True)
        hit = lane == ik
        wk = jnp.sum(jnp.where(hit, scores, 0.0), axis=1, keepdims=True)
        cur = jnp.where(hit, ninf, cur)
        idx = jnp.where(lane_i == k, ik, idx)
        wts = jnp.where(lane_i == k, wk, wts)
        wsum = wsum + wk
    idx_ref[...] = idx.astype(jnp.int32)
    wt_ref[...] = wts / wsum * ROUTED_SCALE


def _route(logits, e_bias):
    T, E = logits.shape
    return pl.pallas_call(
        _route_body,
        name="route",
        grid=(T // ROW_BLK,),
        in_specs=[_row_spec(E), _vec_spec(E)],
        out_specs=[_row_spec(E), _row_spec(E)],
        out_shape=[jax.ShapeDtypeStruct((T, E), jnp.int32), jax.ShapeDtypeStruct((T, E), F32)],
        compiler_params=_params("arbitrary"),
    )(logits, e_bias)


def _ffn_body(x_ref, w1_ref, w3_ref, w2_ref, o_ref):
    x = x_ref[...]
    a = _silu(jnp.dot(x, w1_ref[...], preferred_element_type=F32))
    a = a * jnp.dot(x, w3_ref[...], preferred_element_type=F32)
    o_ref[...] = jnp.dot(a.astype(BF16), w2_ref[...], preferred_element_type=F32)


def _shared_ffn(x, w1, w3, w2):
    T, D = x.shape
    F = w1.shape[1]
    tm = _pick(T, (640, 512, 256, 128))
    return pl.pallas_call(
        _ffn_body,
        name="shared_ffn",
        grid=(T // tm,),
        in_specs=[pl.BlockSpec((tm, D), lambda r: (r, 0)),
                  pl.BlockSpec((D, F), lambda r: (0, 0)), pl.BlockSpec((D, F), lambda r: (0, 0)),
                  pl.BlockSpec((F, D), lambda r: (0, 0))],
        out_specs=pl.BlockSpec((tm, D), lambda r: (r, 0)),
        out_shape=jax.ShapeDtypeStruct((T, D), F32),
        compiler_params=_params("arbitrary"),
    )(x, w1, w3, w2)


def _grouped_ffn_body(be_ref, nu_ref, bv_ref, c0_ref, ta_ref, tb_ref, na_ref, nb_ref, hp_hbm,
                      w1_ref, w3_ref, w2_ref, o_ref, xbuf, sem, h1_ref, h3_ref, act_ref):
    del be_ref
    i = pl.program_id(0)
    p = pl.program_id(1)
    n_used = nu_ref[0]
    used = i < n_used
    n_valid = bv_ref[i]
    slot = lax.rem(i, 2)
    subs = [(s0, min(MOE_SUB, MOE_BLK - s0)) for s0 in range(0, MOE_BLK, MOE_SUB)]

    def row_copy(tok, slot_, r):
        return pltpu.make_async_copy(hp_hbm.at[pl.ds(tok, 1), :], xbuf.at[slot_, pl.ds(r, 1), :], sem.at[slot_])

    def request(blk, slot_, ca_ref, cb_ref, r_lo, n_rows):
        off = lax.rem(c0_ref[blk], TOK_CHUNK)

        def body(t, carry):
            r = r_lo + t
            j = off + r
            tok = jnp.where(j < TOK_CHUNK, ca_ref[0, jnp.minimum(j, TOK_CHUNK - 1)],
                            cb_ref[0, jnp.maximum(j - TOK_CHUNK, 0)])
            row_copy(tok, slot_, r).start()
            return carry

        lax.fori_loop(0, n_rows, body, 0, unroll=ROW_UNROLL)

    @pl.when(jnp.logical_and(i == 0, p == 0))
    def _prime():
        request(0, 0, ta_ref, tb_ref, 0, MOE_BLK)

    @pl.when(i + 1 < n_used)
    def _prefetch():
        q = MOE_BLK // 4
        request(i + 1, 1 - slot, na_ref, nb_ref, p * q, q)

    @pl.when(jnp.logical_and(used, p == 0))
    def _await_rows():
        def body(t, carry):
            row_copy(0, slot, 0).wait()
            return carry

        lax.fori_loop(0, MOE_BLK, body, 0, unroll=ROW_UNROLL)

    @pl.when(jnp.logical_and(used, p < 2))
    def _up():
        w1 = w1_ref[...].astype(BF16)
        w3 = w3_ref[...].astype(BF16)
        for s0, sz in subs:
            @pl.when(s0 < n_valid)
            def _sub(s0=s0, sz=sz):
                xu = xbuf[slot, s0:s0 + sz, :]
                xbits = jnp.where(p == 0, jnp.left_shift(xu, jnp.uint32(16)),
                                  jnp.bitwise_and(xu, jnp.uint32(0xFFFF0000)))
                x = pltpu.bitcast(xbits, F32).astype(BF16)
                a1 = jnp.dot(x, w1, preferred_element_type=F32)
                a3 = jnp.dot(x, w3, preferred_element_type=F32)

                @pl.when(p == 0)
                def _first():
                    h1_ref[s0:s0 + sz, :] = a1
                    h3_ref[s0:s0 + sz, :] = a3

                @pl.when(p == 1)
                def _second():
                    act_ref[s0:s0 + sz, :] = (_silu(h1_ref[s0:s0 + sz, :] + a1)
                                              * (h3_ref[s0:s0 + sz, :] + a3)).astype(BF16)

    @pl.when(jnp.logical_and(used, p >= 2))
    def _down():
        w2 = w2_ref[...].astype(BF16)
        for s0, sz in subs:
            @pl.when(s0 < n_valid)
            def _sub(s0=s0, sz=sz):
                o_ref[s0:s0 + sz, :] = jnp.dot(act_ref[s0:s0 + sz, :], w2, preferred_element_type=F32)

            @pl.when(s0 >= n_valid)
            def _pad(s0=s0, sz=sz):
                o_ref[s0:s0 + sz, :] = jnp.zeros((sz, o_ref.shape[1]), F32)

    @pl.when(jnp.logical_not(used))
    def _unused_block():
        o_ref[...] = jnp.zeros_like(o_ref)


def _grouped_ffn(h_packed, tok_chunks, c0, block_e, n_used, block_valid, w1, w3, w2):
    T, Dh = h_packed.shape
    D = 2 * Dh
    E, _, F = w1.shape
    nb = block_e.shape[0]
    n_chunks = tok_chunks.shape[0]

    def last_used(i, nu):
        return jnp.minimum(i, nu[0] - 1)

    def k_half(i, p, nu):
        return jnp.where(i < nu[0], jnp.minimum(p, 1), 1)

    def n_half(i, p, nu):
        return jnp.where(i < nu[0], jnp.maximum(p - 2, 0), 1)

    def chunk(blk_of, plus):
        def index(i, p, be, nu, bv, c0_):
            q = c0_[last_used(blk_of(i), nu)] // TOK_CHUNK + plus
            return (jnp.minimum(q, n_chunks - 1), 0, 0)
        return pl.BlockSpec((None, 1, TOK_CHUNK), index, memory_space=pltpu.SMEM)

    return pl.pallas_call(
        _grouped_ffn_body,
        name="moe_grouped_ffn",
        grid_spec=pltpu.PrefetchScalarGridSpec(
            num_scalar_prefetch=4,
            grid=(nb, 4),
            in_specs=[
                chunk(lambda i: i, 0), chunk(lambda i: i, 1),
                chunk(lambda i: i + 1, 0), chunk(lambda i: i + 1, 1),
                pl.BlockSpec(memory_space=pl.ANY),
                pl.BlockSpec((None, Dh, F), lambda i, p, be, nu, bv, c0_: (be[i], k_half(i, p, nu), 0)),
                pl.BlockSpec((None, Dh, F), lambda i, p, be, nu, bv, c0_: (be[i], k_half(i, p, nu), 0)),
                pl.BlockSpec((None, F, Dh), lambda i, p, be, nu, bv, c0_: (be[i], 0, n_half(i, p, nu))),
            ],
            out_specs=pl.BlockSpec((MOE_BLK, Dh),
                                   lambda i, p, be, nu, bv, c0_: (jnp.minimum(i, nu[0]),
                                                                  jnp.where(i < nu[0], jnp.maximum(p - 2, 0), 0))),
            scratch_shapes=[pltpu.VMEM((2, MOE_BLK, Dh), jnp.uint32), pltpu.SemaphoreType.DMA((2,)),
                            pltpu.VMEM((MOE_BLK, F), F32), pltpu.VMEM((MOE_BLK, F), F32),
                            pltpu.VMEM((MOE_BLK, F), BF16)],
        ),
        out_shape=jax.ShapeDtypeStruct(((nb + 1) * MOE_BLK, D), F32),
        compiler_params=_params("arbitrary", "arbitrary"),
    )(block_e, n_used, block_valid, c0, tok_chunks, tok_chunks, tok_chunks, tok_chunks, h_packed, w1, w3, w2)


def _moe_routed(h_packed, idx, wts, w1, w3, w2):
    T = h_packed.shape[0]
    E = w1.shape[0]
    A = T * TOP_K
    nb = -(-A // MOE_BLK) + E
    i32 = jnp.int32
    flat_t = jnp.repeat(jnp.arange(T, dtype=i32), TOP_K)
    _, st = lax.sort((idx.reshape(-1), flat_t), num_keys=1, is_stable=True)
    e_iota = jnp.arange(E, dtype=i32)
    onehot = idx[:, :, None] == e_iota[None, None, :]
    csum = jnp.cumsum(onehot.any(axis=1).astype(i32), axis=0)
    counts = csum[-1]
    pcounts = (counts + MOE_BLK - 1) // MOE_BLK * MOE_BLK
    start = jnp.cumsum(counts) - counts
    pend = jnp.cumsum(pcounts)
    pstart = pend - pcounts
    n_used = (pend[-1:] // MOE_BLK).astype(i32)
    blk0 = jnp.arange(nb, dtype=i32) * MOE_BLK
    block_e = jnp.minimum(jnp.searchsorted(pend, blk0, side='right'), E - 1).astype(i32)
    block_e = block_e[jnp.minimum(jnp.arange(nb), n_used[0] - 1)]
    off_in_e = blk0 - pstart[block_e]
    block_valid = jnp.clip(counts[block_e] - off_in_e, 0, MOE_BLK).astype(i32)
    c0 = jnp.clip(start[block_e] + off_in_e, 0, A).astype(i32)
    n_chunks = -(-A // TOK_CHUNK) + 1
    tok_chunks = jnp.pad(st, (0, n_chunks * TOK_CHUNK - A)).reshape(n_chunks, 1, TOK_CHUNK)
    y_sorted = _grouped_ffn(h_packed, tok_chunks, c0, block_e, n_used, block_valid, w1, w3, w2)
    slot = (pstart[None, :] + csum - 1)[:, None, :]
    dest = jnp.sum(jnp.where(onehot, slot, 0), axis=-1)
    y_tok = jnp.take(y_sorted, dest, axis=0, mode="clip")
    return jnp.sum(y_tok * wts[:, :, None], axis=1)


def _rope_tables(pos):
    half = ROT_DIM // 2
    inv = jnp.power(ROPE_THETA, -jnp.arange(half, dtype=F32) / half)
    ang = pos.astype(F32)[:, None] * inv[None, :]
    cos, sin = jnp.cos(ang), jnp.sin(ang)
    n = pos.shape[0]
    ones = jnp.ones((n, HEAD_DIM - ROT_DIM), F32)
    zeros = jnp.zeros((n, HEAD_DIM - half), F32)
    cos_t = jnp.concatenate([cos, cos, ones], axis=1)
    sin_a = jnp.concatenate([-sin, zeros], axis=1)
    sin_b = jnp.concatenate([jnp.zeros((n, half), F32), sin, jnp.zeros((n, HEAD_DIM - ROT_DIM), F32)], axis=1)
    return cos_t, sin_a, sin_b


def kernel(x_prompt, x_sample, c_prompt, c_sample, cache_k, cache_v, state_ssm, state_conv, page_table, w_ada, b_ada, norm_mix_pre, norm_mix_post, norm_ffn_pre, norm_ffn_post, w_in, conv_w, a_log, dt_bias, gdn_norm, w_out, w_router, e_bias, w1, w3, w2, ws1, ws3, ws2):
    B, S, D = x_prompt.shape
    Bs = x_sample.shape[0]
    assert x_sample.shape[1] == 1 and w_ada.shape[0] == 1
    assert Bs == ROW_BLK and S % MOBA_BLOCK == 0 and S % ROW_BLK == 0
    Tp = B * S
    T = Tp + Bs
    rb_per_seq = S // ROW_BLK
    past_len = page_table.shape[1] * PAGE_SIZE
    H = GDN_HEADS

    x_all = jnp.concatenate([x_prompt.reshape(Tp, D), x_sample.reshape(Bs, D)], axis=0)

    c_all = jnp.concatenate([c_prompt, c_sample], axis=0)
    n_c = c_all.shape[0]
    c_pad = jnp.pad(c_all, ((0, (-n_c) % 8), (0, 0)))
    mod = _matmul(c_pad, w_ada[0], b_ada, name="adaln", silu_in=True, tm=c_pad.shape[0])
    mod_rows = jnp.concatenate([jnp.repeat(mod[:B], ROW_BLK, axis=0), mod[B:B + Bs]], axis=0)

    h = _prenorm(x_all, norm_mix_pre, mod_rows, 0, 1, rb_per_seq)
    w_in0 = w_in[0]
    proj = _matmul(h, w_in0, name="in_proj", n_cols=OFF_AB)
    w_ab = jnp.pad(w_in0[:, OFF_AB:], ((0, 0), (0, HEAD_DIM - 2 * H)))
    ab = _matmul(h, w_ab, name="in_proj_gates")

    pos = jnp.concatenate([jnp.tile(jnp.arange(S, dtype=jnp.int32), B),
                           jnp.full((Bs,), past_len, jnp.int32)])
    qk_rot = _rope(proj, *_rope_tables(pos))
    k_rot = qk_rot[:, Q_ATTN:]
    v_new = proj[:, OFF_V:OFF_V + KV_ATTN]
    o_a_p = _attn_prefill(qk_rot, proj, B, S)
    grp = ATTN_HEADS // KV_HEADS
    q_s = qk_rot[Tp:, :Q_ATTN].reshape(Bs, ATTN_HEADS, HEAD_DIM)
    kx = jnp.repeat(k_rot[Tp:].reshape(Bs, KV_HEADS, HEAD_DIM), grp, axis=1)
    vx = jnp.repeat(v_new[Tp:].reshape(Bs, KV_HEADS, HEAD_DIM), grp, axis=1)
    o_a_s = _attn_decode(q_s, kx, vx, cache_k.reshape(-1, HEAD_DIM), cache_v.reshape(-1, HEAD_DIM), page_table)
    o_a = jnp.concatenate([o_a_p, o_a_s.reshape(Bs, Q_ATTN)], axis=0)

    alog_pad = jnp.pad(a_log.astype(F32), ((0, 0), (0, HEAD_DIM - H)))
    dtb_pad = jnp.pad(dt_bias.astype(F32), ((0, 0), (0, HEAD_DIM - H)))
    gb = _gates(ab, alog_pad, dtb_pad)
    qkv_p = _conv_prompt(proj, conv_w[0], B, S)
    buf_t = state_conv[0].transpose(1, 0, 2)
    qkv_s = _conv_sample(proj, buf_t, conv_w[0], Tp // Bs)
    n_chunks = S // GDN_CHUNK

    def chunk_rows(a):
        return a.reshape(B, n_chunks, GDN_CHUNK, H).transpose(0, 1, 3, 2)

    o_g_p, ssm_p = _gdn_prompt(qkv_p, chunk_rows(gb[:Tp, :H]), chunk_rows(gb[:Tp, H:2 * H]), B, S)
    o_g_s, ssm_s = _gdn_sample(qkv_s.reshape(Bs, 3 * H, HEAD_DIM), gb[Tp:].reshape(Bs, 1, HEAD_DIM),
                               state_ssm[0].astype(F32))
    o_g = jnp.concatenate([o_g_p, o_g_s.reshape(Bs, V_GDN)], axis=0)
    o_gn = _gated_norm(o_g, proj, gdn_norm)

    mix = _matmul(jnp.concatenate([o_a, o_gn], axis=1), w_out[0], name="out_proj")

    x1, h2, h2_packed, logits = _mid(x_all, mix, norm_mix_post, norm_ffn_pre, mod_rows, w_router[0], rb_per_seq)
    idx_l, wts_l = _route(logits, e_bias)
    routed = _moe_routed(h2_packed, idx_l[:, :TOP_K], wts_l[:, :TOP_K], w1[0], w3[0], w2[0])
    shared = _shared_ffn(h2, ws1[0].astype(BF16), ws3[0].astype(BF16), ws2[0].astype(BF16))
    y_all = _final(x1, routed, shared, norm_ffn_post, mod_rows, rb_per_seq)

    y_prompt = y_all[:Tp].reshape(B, S, D)
    y_sample = y_all[Tp:].reshape(Bs, 1, D)
    k_prompt = k_rot[:Tp].reshape(1, B, S, KV_HEADS, HEAD_DIM)
    v_prompt = v_new[:Tp].reshape(1, B, S, KV_HEADS, HEAD_DIM)
    k_sample = k_rot[Tp:].reshape(1, Bs, 1, KV_HEADS, HEAD_DIM)
    v_sample = v_new[Tp:].reshape(1, Bs, 1, KV_HEADS, HEAD_DIM)
    def pre_conv_rows(r0, r1):
        return lax.slice(proj, (r0, OFF_CONV), (r1, OFF_CONV + GDN_CONV_CH))

    conv_prompt = jnp.stack([pre_conv_rows((b + 1) * S - (CONV_W - 1), (b + 1) * S) for b in range(B)])[None]
    conv_sample = jnp.concatenate([state_conv[0][:, 1:], pre_conv_rows(Tp, T)[:, None, :]], axis=1)[None]
    return (y_prompt, y_sample, k_prompt, v_prompt, k_sample, v_sample,
            ssm_p[None].astype(state_ssm.dtype), ssm_s[None].astype(state_ssm.dtype),
            conv_prompt.astype(state_conv.dtype), conv_sample.astype(state_conv.dtype))
```

```python
import functools

import jax
import jax.numpy as jnp
from jax import lax
from jax.experimental import pallas as pl
from jax.experimental.pallas import tpu as pltpu

F32 = jnp.float32
BF16 = jnp.bfloat16
HIGHEST = lax.Precision.HIGHEST

HEAD_DIM = 128
ATTN_HEADS = 16
KV_HEADS = 4
GDN_HEADS = 16
CONV_W = 4
GDN_CHUNK = 64
MOBA_BLOCK = 256
MOBA_TOPK = 3
ROPE_THETA = 500000.0
ROT_DIM = HEAD_DIM // 4
PAGE_SIZE = 128
N_GROUPS = 8
TOPK_GROUPS = 4
TOP_K = 8
ROUTED_SCALE = 2.5
NORM_EPS = 1e-6

Q_ATTN = ATTN_HEADS * HEAD_DIM
KV_ATTN = KV_HEADS * HEAD_DIM
QK_GDN = GDN_HEADS * HEAD_DIM
V_GDN = GDN_HEADS * HEAD_DIM
GDN_CONV_CH = 2 * QK_GDN + V_GDN
OFF_K = Q_ATTN
OFF_V = Q_ATTN + KV_ATTN
OFF_CONV = Q_ATTN + 2 * KV_ATTN
OFF_Z = OFF_CONV + GDN_CONV_CH
OFF_AB = OFF_Z + V_GDN

ROW_BLK = 128
MOE_BLK = 576
MOE_SUB = 288
TOK_CHUNK = 1024
ROW_UNROLL = 8
NEG = -1e30
VMEM_LIMIT = 56 << 20


def _params(*sem):
    return pltpu.CompilerParams(dimension_semantics=sem, vmem_limit_bytes=VMEM_LIMIT)


def _pick(n, prefs):
    for p in prefs:
        if n % p == 0:
            return p
    return n


def _nt(a, b, **kw):
    return lax.dot_general(a, b, (((1,), (1,)), ((), ())), preferred_element_type=F32, **kw)


def _silu(x):
    return x * jax.nn.sigmoid(x)


def _rms(x, g):
    return x * lax.rsqrt(jnp.mean(x * x, axis=-1, keepdims=True) + NORM_EPS) * g


def _beats(a, b, a_first):
    return jnp.where(a > b, 1.0, jnp.where(a == b, jnp.where(a_first, 1.0, 0.0), 0.0))


def _mm_body(x_ref, w_ref, b_ref, o_ref, wbf_ref, *, silu_in):
    @pl.when(pl.program_id(1) == 0)
    def _cast_weight_tile():
        wbf_ref[...] = w_ref[...].astype(BF16)

    x = x_ref[...]
    if silu_in:
        x = _silu(x.astype(F32))
    acc = jnp.dot(x.astype(BF16), wbf_ref[...], preferred_element_type=F32)
    o_ref[...] = (acc + b_ref[...]).astype(o_ref.dtype)


def _matmul(x, w, b=None, *, name, n_cols=None, silu_in=False, out_dtype=F32, tm=None, tn=None):
    M, K = x.shape
    N = n_cols or w.shape[1]
    tm = tm or _pick(M, (640, 512, 256, 128))
    tn = tn or _pick(N, (512, 256, 128))
    if b is None:
        b = jnp.zeros((1, N), F32)
    return pl.pallas_call(
        functools.partial(_mm_body, silu_in=silu_in),
        name=name,
        grid=(N // tn, M // tm),
        in_specs=[
            pl.BlockSpec((tm, K), lambda n, m: (m, 0)),
            pl.BlockSpec((K, tn), lambda n, m: (0, n)),
            pl.BlockSpec((1, tn), lambda n, m: (0, n)),
        ],
        out_specs=pl.BlockSpec((tm, tn), lambda n, m: (m, n)),
        out_shape=jax.ShapeDtypeStruct((M, N), out_dtype),
        scratch_shapes=[pltpu.VMEM((K, tn), BF16)],
        compiler_params=_params("arbitrary", "arbitrary"),
    )(x, w, b)


def _prenorm_body(x_ref, g_ref, sh_ref, sc_ref, o_ref):
    y = _rms(x_ref[...], g_ref[...]) * (1.0 + sc_ref[...]) + sh_ref[...]
    o_ref[...] = y.astype(o_ref.dtype)


def _mod_spec(D, slot, rb_per_seq):
    return pl.BlockSpec((ROW_BLK, D), lambda r: (r // rb_per_seq, slot))


def _row_spec(D):
    return pl.BlockSpec((ROW_BLK, D), lambda r: (r, 0))


def _vec_spec(D):
    return pl.BlockSpec((1, D), lambda r: (0, 0))


def _prenorm(x, g, mod, slot_sh, slot_sc, rb_per_seq):
    T, D = x.shape
    return pl.pallas_call(
        _prenorm_body,
        name="prenorm",
        grid=(T // ROW_BLK,),
        in_specs=[_row_spec(D), _vec_spec(D), _mod_spec(D, slot_sh, rb_per_seq),
                  _mod_spec(D, slot_sc, rb_per_seq)],
        out_specs=_row_spec(D),
        out_shape=jax.ShapeDtypeStruct((T, D), BF16),
        compiler_params=_params("arbitrary"),
    )(x, g, mod, mod)


def _mid_body(x_ref, mix_ref, gpost_ref, gt_ref, gpre_ref, sh_ref, sc_ref, wr_ref,
              x1_ref, h_ref, hp_ref, lg_ref):
    x1 = x_ref[...] + gt_ref[...] * _rms(mix_ref[...], gpost_ref[...])
    x1_ref[...] = x1
    h = _rms(x1, gpre_ref[...]) * (1.0 + sc_ref[...]) + sh_ref[...]
    hb = h.astype(BF16)
    h_ref[...] = hb
    bits = pltpu.bitcast(hb.astype(F32), jnp.uint32)
    dh = bits.shape[1] // 2
    hp_ref[...] = jnp.bitwise_or(jnp.bitwise_and(bits[:, dh:], jnp.uint32(0xFFFF0000)),
                                 jnp.right_shift(bits[:, :dh], jnp.uint32(16)))
    lg_ref[...] = jnp.dot(h, wr_ref[...], precision=HIGHEST, preferred_element_type=F32)


def _mid(x, mix, g_post, g_pre, mod, w_router, rb_per_seq):
    T, D = x.shape
    E = w_router.shape[1]
    return pl.pallas_call(
        _mid_body,
        name="mid_norms_router",
        grid=(T // ROW_BLK,),
        in_specs=[_row_spec(D), _row_spec(D), _vec_spec(D), _mod_spec(D, 2, rb_per_seq),
                  _vec_spec(D), _mod_spec(D, 3, rb_per_seq), _mod_spec(D, 4, rb_per_seq),
                  pl.BlockSpec((D, E), lambda r: (0, 0))],
        out_specs=[_row_spec(D), _row_spec(D), _row_spec(D // 2), _row_spec(E)],
        out_shape=[jax.ShapeDtypeStruct((T, D), F32), jax.ShapeDtypeStruct((T, D), BF16),
                   jax.ShapeDtypeStruct((T, D // 2), jnp.uint32), jax.ShapeDtypeStruct((T, E), F32)],
        compiler_params=_params("arbitrary"),
    )(x, mix, g_post, mod, g_pre, mod, mod, w_router)


def _final_body(x1_ref, a_ref, b_ref, g_ref, gt_ref, o_ref):
    f = a_ref[...] + b_ref[...]
    o_ref[...] = x1_ref[...] + gt_ref[...] * _rms(f, g_ref[...])


def _final(x1, routed, shared, g_post, mod, rb_per_seq):
    T, D = x1.shape
    return pl.pallas_call(
        _final_body,
        name="final_norm",
        grid=(T // ROW_BLK,),
        in_specs=[_row_spec(D), _row_spec(D), _row_spec(D), _vec_spec(D),
                  _mod_spec(D, 5, rb_per_seq)],
        out_specs=_row_spec(D),
        out_shape=jax.ShapeDtypeStruct((T, D), F32),
        compiler_params=_params("arbitrary"),
    )(x1, routed, shared, g_post, mod)


def _rope_body(x_ref, c_ref, sa_ref, sb_ref, o_ref):
    x = x_ref[...]
    half = ROT_DIM // 2
    o_ref[...] = (x * c_ref[...] + pltpu.roll(x, HEAD_DIM - half, 1) * sa_ref[...]
                  + pltpu.roll(x, half, 1) * sb_ref[...])


def _rope(proj, cos_t, sin_a, sin_b):
    T = proj.shape[0]
    n_heads = ATTN_HEADS + KV_HEADS
    tr = _pick(T, (640, 512, 256, 128))
    tab = pl.BlockSpec((tr, HEAD_DIM), lambda r, h: (r, 0))
    blk = pl.BlockSpec((tr, HEAD_DIM), lambda r, h: (r, h))
    return pl.pallas_call(
        _rope_body,
        name="rope",
        grid=(T // tr, n_heads),
        in_specs=[blk, tab, tab, tab],
        out_specs=blk,
        out_shape=jax.ShapeDtypeStruct((T, n_heads * HEAD_DIM), F32),
        compiler_params=_params("arbitrary", "arbitrary"),
    )(proj, cos_t, sin_a, sin_b)


def _topk_mask(gate, lane, n_valid, n_cand):
    gate = jnp.where(lane < n_valid, gate, -jnp.inf)
    rank = jnp.zeros(gate.shape, F32)
    for m in range(n_cand):
        rank = rank + _beats(gate[:, m:m + 1], gate, m < lane)
    return jnp.where(lane < n_valid, jnp.where(rank < MOBA_TOPK, 1.0, 0.0), 0.0)


def _attn_prefill_body(q_ref, k_ref, v_ref, o_ref, s_ref, mx_ref, ls_ref, acc_ref, sel_ref, *, nblk, grp):
    i = pl.program_id(2)
    blk = MOBA_BLOCK
    rows = grp * blk
    scale = HEAD_DIM ** -0.5
    qf = jnp.concatenate([q_ref[:, h * HEAD_DIM:(h + 1) * HEAD_DIM] for h in range(grp)], axis=0)
    qb = qf.astype(BF16)

    krow = lax.broadcasted_iota(jnp.int32, (HEAD_DIM, HEAD_DIM), 0)
    kmean = jnp.zeros((HEAD_DIM, HEAD_DIM), F32)
    for n in range(nblk):
        kmean = jnp.where(krow == n, jnp.mean(k_ref[n * blk:(n + 1) * blk, :], axis=0, keepdims=True), kmean)
    gate = _nt(qf, kmean, precision=HIGHEST)
    lane = lax.broadcasted_iota(jnp.int32, (rows, HEAD_DIM), 1)
    sel_ref[...] = _topk_mask(gate, lane, i, nblk)

    start = pl.multiple_of(i * blk, blk)
    s = _nt(qb, k_ref[pl.ds(start, blk), :].astype(BF16)) * scale
    qpos = jnp.bitwise_and(lax.broadcasted_iota(jnp.int32, (rows, blk), 0), blk - 1)
    kpos = lax.broadcasted_iota(jnp.int32, (rows, blk), 1)
    s = jnp.where(kpos <= qpos, s, NEG)
    s_ref[nblk - 1] = s
    mx_ref[...] = s

    for n in range(nblk - 1):
        @pl.when(n < i)
        def _scores(n=n):
            s = _nt(qb, k_ref[n * blk:(n + 1) * blk, :].astype(BF16)) * scale
            s = jnp.where(sel_ref[:, n:n + 1] > 0.0, s, NEG)
            s_ref[n] = s
            mx_ref[...] = jnp.maximum(mx_ref[...], s)

    m = jnp.max(mx_ref[...], axis=1, keepdims=True)
    p = jnp.exp(s_ref[nblk - 1] - m)
    ls_ref[...] = p
    acc_ref[...] = jnp.dot(p.astype(BF16), v_ref[pl.ds(start, blk), :].astype(BF16),
                           preferred_element_type=F32)

    for n in range(nblk - 1):
        @pl.when(n < i)
        def _weights(n=n):
            p = jnp.exp(s_ref[n] - m)
            ls_ref[...] = ls_ref[...] + p
            acc_ref[...] = acc_ref[...] + jnp.dot(
                p.astype(BF16), v_ref[n * blk:(n + 1) * blk, :].astype(BF16),
                preferred_element_type=F32)

    out = acc_ref[...] / jnp.sum(ls_ref[...], axis=1, keepdims=True)
    for h in range(grp):
        o_ref[:, h * HEAD_DIM:(h + 1) * HEAD_DIM] = out[h * blk:(h + 1) * blk].astype(o_ref.dtype)


def _attn_prefill(qk_rot, proj, B, S):
    nblk = S // MOBA_BLOCK
    grp = ATTN_HEADS // KV_HEADS
    rows = grp * MOBA_BLOCK
    k_col0 = OFF_K // HEAD_DIM
    v_col0 = OFF_V // HEAD_DIM
    return pl.pallas_call(
        functools.partial(_attn_prefill_body, nblk=nblk, grp=grp),
        name="attn_prefill",
        grid=(B, KV_HEADS, nblk),
        in_specs=[
            pl.BlockSpec((MOBA_BLOCK, grp * HEAD_DIM), lambda b, g, i: (b * nblk + i, g)),
            pl.BlockSpec((S, HEAD_DIM), lambda b, g, i: (b, k_col0 + g)),
            pl.BlockSpec((S, HEAD_DIM), lambda b, g, i: (b, v_col0 + g)),
        ],
        out_specs=pl.BlockSpec((MOBA_BLOCK, grp * HEAD_DIM), lambda b, g, i: (b * nblk + i, g)),
        out_shape=jax.ShapeDtypeStruct((B * S, Q_ATTN), BF16),
        scratch_shapes=[pltpu.VMEM((nblk, rows, MOBA_BLOCK), F32), pltpu.VMEM((rows, MOBA_BLOCK), F32),
                        pltpu.VMEM((rows, MOBA_BLOCK), F32), pltpu.VMEM((rows, HEAD_DIM), F32),
                        pltpu.VMEM((rows, HEAD_DIM), F32)],
        compiler_params=_params("arbitrary", "arbitrary", "arbitrary"),
    )(qk_rot, qk_rot, proj)


def _attn_decode_body(pt_ref, q_ref, kx_ref, vx_ref, *refs, n_pages):
    del pt_ref
    k_pages, v_pages, o_ref = refs[:n_pages], refs[n_pages:2 * n_pages], refs[2 * n_pages]
    H = ATTN_HEADS
    grp = ATTN_HEADS // KV_HEADS
    ppb = MOBA_BLOCK // PAGE_SIZE
    nblk = n_pages // ppb
    page_rows = PAGE_SIZE * KV_HEADS
    blk_cols = MOBA_BLOCK * KV_HEADS
    ncol = n_pages * page_rows
    scale = HEAD_DIM ** -0.5
    q = q_ref[...]
    row1 = lax.broadcasted_iota(jnp.int32, (H, 1), 0)
    lane = lax.broadcasted_iota(jnp.int32, (H, HEAD_DIM), 1)

    gate = jnp.zeros((H, HEAD_DIM), F32)
    for n in range(nblk):
        ks8 = k_pages[n * ppb][...].reshape(page_rows // 8, 8, HEAD_DIM).sum(axis=0)
        for j in range(1, ppb):
            ks8 = ks8 + k_pages[n * ppb + j][...].reshape(page_rows // 8, 8, HEAD_DIM).sum(axis=0)
        ksum = ks8[:KV_HEADS] + ks8[KV_HEADS:]
        kexp = jnp.zeros((H, HEAD_DIM), F32)
        for h in range(KV_HEADS):
            mine_h = jnp.logical_and(row1 >= h * grp, row1 < (h + 1) * grp)
            kexp = jnp.where(mine_h, ksum[h:h + 1, :], kexp)
        g_n = jnp.sum(q * kexp, axis=1, keepdims=True) / MOBA_BLOCK
        gate = jnp.where(lane == n, g_n, gate)
    sel = _topk_mask(gate, lane, nblk, nblk)

    kall = jnp.concatenate([kp[...].astype(BF16) for kp in k_pages], axis=0)
    s = _nt(q.astype(BF16), kall) * scale
    col = lax.broadcasted_iota(jnp.int32, (H, ncol), 1)
    row = lax.broadcasted_iota(jnp.int32, (H, ncol), 0)
    own_head = jnp.bitwise_and(col, KV_HEADS - 1) * grp == jnp.bitwise_and(row, -grp)
    picked = jnp.concatenate(
        [jnp.broadcast_to(sel[:, n:n + 1], (H, blk_cols)) for n in range(nblk)], axis=1)
    s = jnp.where(jnp.logical_and(own_head, picked > 0.0), s, NEG)
    s_own = jnp.sum(q * kx_ref[...], axis=1, keepdims=True) * scale
    m = jnp.maximum(jnp.max(s, axis=1, keepdims=True), s_own)
    p = jnp.exp(s - m)
    e_own = jnp.exp(s_own - m)
    denom = jnp.sum(p, axis=1, keepdims=True) + e_own
    vall = jnp.concatenate([vp[...].astype(BF16) for vp in v_pages], axis=0)
    out = (jnp.dot(p.astype(BF16), vall, preferred_element_type=F32) + e_own * vx_ref[...]) / denom
    o_ref[...] = out.astype(o_ref.dtype)


def _attn_decode(q_s, kx, vx, cache_k, cache_v, page_table):
    Bs, n_pages = page_table.shape
    assert n_pages % (MOBA_BLOCK // PAGE_SIZE) == 0
    assert KV_HEADS & (KV_HEADS - 1) == 0 and (ATTN_HEADS // KV_HEADS) & (ATTN_HEADS // KV_HEADS - 1) == 0
    vec = pl.BlockSpec((None, ATTN_HEADS, HEAD_DIM), lambda b, pt: (b, 0, 0))

    def page(j):
        return pl.BlockSpec((PAGE_SIZE * KV_HEADS, HEAD_DIM), lambda b, pt: (pt[b, j], 0))

    pages = [page(j) for j in range(n_pages)]
    return pl.pallas_call(
        functools.partial(_attn_decode_body, n_pages=n_pages),
        name="attn_decode",
        grid_spec=pltpu.PrefetchScalarGridSpec(
            num_scalar_prefetch=1,
            grid=(Bs,),
            in_specs=[vec, vec, vec] + pages + pages,
            out_specs=vec,
        ),
        out_shape=jax.ShapeDtypeStruct((Bs, ATTN_HEADS, HEAD_DIM), BF16),
        compiler_params=_params("arbitrary"),
    )(page_table, q_s, kx, vx, *([cache_k] * n_pages), *([cache_v] * n_pages))


def _conv_finish(y, c):
    y = _silu(y)
    nrm = y * lax.rsqrt(jnp.sum(y * y, axis=1, keepdims=True) + 1e-6)
    return jnp.where(c < GDN_HEADS, nrm * (HEAD_DIM ** -0.5), jnp.where(c < 2 * GDN_HEADS, nrm, y))


def _conv_prompt_body(x_ref, w_ref, o_ref):
    x = x_ref[...]
    row = lax.broadcasted_iota(jnp.int32, x.shape, 0)
    y = x * w_ref[CONV_W - 1:CONV_W, :]
    for k in range(1, CONV_W):
        y = y + jnp.where(row >= k, pltpu.roll(x, k, 0), 0.0) * w_ref[CONV_W - 1 - k:CONV_W - k, :]
    o_ref[...] = _conv_finish(y, pl.program_id(1))


def _conv_prompt(proj, conv_w, B, S):
    n_tiles = GDN_CONV_CH // HEAD_DIM
    col0 = OFF_CONV // HEAD_DIM
    return pl.pallas_call(
        _conv_prompt_body,
        name="conv_prompt",
        grid=(B, n_tiles),
        in_specs=[pl.BlockSpec((S, HEAD_DIM), lambda b, c: (b, col0 + c)),
                  pl.BlockSpec((CONV_W, HEAD_DIM), lambda b, c: (0, c))],
        out_specs=pl.BlockSpec((S, HEAD_DIM), lambda b, c: (b, c)),
        out_shape=jax.ShapeDtypeStruct((B * S, GDN_CONV_CH), F32),
        compiler_params=_params("arbitrary", "arbitrary"),
    )(proj, conv_w)


def _conv_sample_body(x_ref, buf_ref, w_ref, o_ref):
    y = x_ref[...] * w_ref[CONV_W - 1:CONV_W, :]
    for i in range(CONV_W - 1):
        y = y + buf_ref[i] * w_ref[i:i + 1, :]
    o_ref[...] = _conv_finish(y, pl.program_id(0))


def _conv_sample(proj, buf_t, conv_w, row_blk0):
    Bs = buf_t.shape[1]
    n_tiles = GDN_CONV_CH // HEAD_DIM
    col0 = OFF_CONV // HEAD_DIM
    return pl.pallas_call(
        _conv_sample_body,
        name="conv_sample",
        grid=(n_tiles,),
        in_specs=[pl.BlockSpec((Bs, HEAD_DIM), lambda c: (row_blk0, col0 + c)),
                  pl.BlockSpec((CONV_W - 1, Bs, HEAD_DIM), lambda c: (0, 0, c)),
                  pl.BlockSpec((CONV_W, HEAD_DIM), lambda c: (0, c))],
        out_specs=pl.BlockSpec((Bs, HEAD_DIM), lambda c: (0, c)),
        out_shape=jax.ShapeDtypeStruct((Bs, GDN_CONV_CH), F32),
        compiler_params=_params("arbitrary"),
    )(proj, buf_t, conv_w)


def _gates_body(ab_ref, alog_ref, dtb_ref, o_ref):
    x = ab_ref[...]
    sp_in = x + dtb_ref[...]
    softplus = jnp.maximum(sp_in, 0.0) + jnp.log1p(jnp.exp(-jnp.abs(sp_in)))
    g = -jnp.exp(alog_ref[...]) * softplus
    lane = lax.broadcasted_iota(jnp.int32, x.shape, 1)
    o_ref[...] = jnp.where(lane < GDN_HEADS, g, jax.nn.sigmoid(x))


def _gates(ab, alog_pad, dtb_pad):
    T = ab.shape[0]
    return pl.pallas_call(
        _gates_body,
        name="gdn_gates",
        grid=(T // ROW_BLK,),
        in_specs=[_row_spec(HEAD_DIM), _vec_spec(HEAD_DIM), _vec_spec(HEAD_DIM)],
        out_specs=_row_spec(HEAD_DIM),
        out_shape=jax.ShapeDtypeStruct((T, HEAD_DIM), F32),
        compiler_params=_params("arbitrary"),
    )(ab, alog_pad, dtb_pad)


def _mmb(a, b):
    return jnp.dot(a.astype(BF16), b.astype(BF16), preferred_element_type=F32)


def _split_bf16(a):
    hi = a.astype(BF16)
    return hi, (a - hi.astype(F32)).astype(BF16)


def _mm3(a, b):
    ah, al = _split_bf16(a)
    bh, bl = _split_bf16(b)

    def d(x, y):
        return jnp.dot(x, y, preferred_element_type=F32)

    return d(ah, bh) + (d(ah, bl) + d(al, bh))


def _unit_lower_inverse(lows):
    C = lows[0].shape[0]
    ii = lax.broadcasted_iota(jnp.int32, (C, C), 0)
    jj = lax.broadcasted_iota(jnp.int32, (C, C), 1)
    eye = jnp.where(ii == jj, 1.0, 0.0)
    invs = [eye - low for low in lows]
    pws = [_mm3(low, low) for low in lows]
    span = 2
    while span < C:
        invs = [inv + _mm3(inv, pw) for inv, pw in zip(invs, pws)]
        span *= 2
        if span < C:
            pws = [_mm3(pw, pw) for pw in pws]
    return invs


GDN_HEAD_GROUP = 8


def _gdn_chunk_body(q_ref, k_ref, v_ref, g_ref, b_ref, o_ref, s_out_ref, s_ref):
    n = pl.program_id(1)
    C = GDN_CHUNK

    @pl.when(n == 0)
    def _zero_state():
        s_ref[...] = jnp.zeros_like(s_ref)

    ii = lax.broadcasted_iota(jnp.int32, (C, C), 0)
    jj = lax.broadcasted_iota(jnp.int32, (C, C), 1)
    eye = ii == jj
    tril = jj <= ii
    for h0 in range(0, GDN_HEADS, GDN_HEAD_GROUP):
        hs = range(h0, h0 + GDN_HEAD_GROUP)
        cols = [slice(h * HEAD_DIM, (h + 1) * HEAD_DIM) for h in hs]
        q = [q_ref[:, c] for c in cols]
        k = [k_ref[:, c] for c in cols]
        v = [v_ref[:, c] for c in cols]
        gc_col = [jnp.sum(jnp.where(tril, g_ref[h:h + 1, :], 0.0), axis=1, keepdims=True) for h in hs]
        gc_row = [jnp.sum(jnp.where(eye, x, 0.0), axis=0, keepdims=True) for x in gc_col]
        beta = [jnp.sum(jnp.where(eye, b_ref[h:h + 1, :], 0.0), axis=1, keepdims=True) for h in hs]
        decay = [jnp.exp(jnp.where(tril, c_ - r_, -jnp.inf)) for c_, r_ in zip(gc_col, gc_row)]
        g_last = [x[C - 1:C, :] for x in gc_col]
        e_gc = [jnp.exp(x) for x in gc_col]
        kbeta = [k_ * b_ for k_, b_ in zip(k, beta)]
        k16 = [x.astype(BF16) for x in k]
        low = [jnp.where(ii > jj, _nt(kb.astype(BF16), kk) * d, 0.0) for kb, kk, d in zip(kbeta, k16, decay)]
        tmat = [t.astype(BF16) for t in _unit_lower_inverse(low)]
        u = [_mmb(t, v_ * b_) for t, v_, b_ in zip(tmat, v, beta)]
        w = [_mmb(t, kb * e) for t, kb, e in zip(tmat, kbeta, e_gc)]
        attn = [_nt(q_.astype(BF16), kk) * d for q_, kk, d in zip(q, k16, decay)]
        state = [s_ref[h] for h in hs]
        v_new = [u_ - _mmb(w_, s_) for u_, w_, s_ in zip(u, w, state)]
        for c, q_, e, s_, a_, vn in zip(cols, q, e_gc, state, attn, v_new):
            o_ref[:, c] = _mmb(q_ * e, s_) + _mmb(a_, vn)
        for h, k_, gl, gc, s_, vn in zip(hs, k, g_last, gc_col, state, v_new):
            k_tail = k_ * jnp.exp(gl - gc)
            s_ref[h] = s_ * jnp.exp(gl) + _mmb(k_tail.T, vn)

    @pl.when(n == pl.num_programs(1) - 1)
    def _emit_state():
        s_out_ref[...] = s_ref[...]


def _gdn_prompt(qkv, g_rows, b_rows, B, S):
    N = S // GDN_CHUNK
    W = GDN_HEADS * HEAD_DIM

    def part(j):
        return pl.BlockSpec((GDN_CHUNK, W), lambda b, n: (b * N + n, j))

    gspec = pl.BlockSpec((None, None, GDN_HEADS, GDN_CHUNK), lambda b, n: (b, n, 0, 0))
    return pl.pallas_call(
        _gdn_chunk_body,
        name="gdn_prompt",
        grid=(B, N),
        in_specs=[part(0), part(1), part(2), gspec, gspec],
        out_specs=[pl.BlockSpec((GDN_CHUNK, W), lambda b, n: (b * N + n, 0)),
                   pl.BlockSpec((None, GDN_HEADS, HEAD_DIM, HEAD_DIM), lambda b, n: (b, 0, 0, 0))],
        out_shape=[jax.ShapeDtypeStruct((B * S, W), F32),
                   jax.ShapeDtypeStruct((B, GDN_HEADS, HEAD_DIM, HEAD_DIM), F32)],
        scratch_shapes=[pltpu.VMEM((GDN_HEADS, HEAD_DIM, HEAD_DIM), F32)],
        compiler_params=_params("arbitrary", "arbitrary"),
    )(qkv, qkv, qkv, g_rows, b_rows)


def _gdn_step_body(qkv_ref, gb_ref, s_ref, o_ref, s_out_ref):
    H = GDN_HEADS
    ii = lax.broadcasted_iota(jnp.int32, (HEAD_DIM, HEAD_DIM), 0)
    jj = lax.broadcasted_iota(jnp.int32, (HEAD_DIM, HEAD_DIM), 1)
    eye = ii == jj
    lane = lax.broadcasted_iota(jnp.int32, (1, HEAD_DIM), 1)
    gb = gb_ref[...]
    for h in range(H):
        q_col = jnp.sum(jnp.where(eye, qkv_ref[h:h + 1, :], 0.0), axis=1, keepdims=True)
        k_col = jnp.sum(jnp.where(eye, qkv_ref[H + h:H + h + 1, :], 0.0), axis=1, keepdims=True)
        v_row = qkv_ref[2 * H + h:2 * H + h + 1, :]
        g = jnp.sum(jnp.where(lane == h, gb, 0.0), axis=1, keepdims=True)
        beta = jnp.sum(jnp.where(lane == H + h, gb, 0.0), axis=1, keepdims=True)
        state = s_ref[h] * jnp.exp(g)
        kv = jnp.sum(k_col * state, axis=0, keepdims=True)
        delta = (v_row - kv) * beta
        state = state + k_col * delta
        s_out_ref[h] = state
        o_ref[h:h + 1, :] = jnp.sum(q_col * state, axis=0, keepdims=True)


def _gdn_sample(qkv3, gb3, state):
    Bs = qkv3.shape[0]
    H = GDN_HEADS
    sspec = pl.BlockSpec((None, H, HEAD_DIM, HEAD_DIM), lambda b: (b, 0, 0, 0))
    return pl.pallas_call(
        _gdn_step_body,
        name="gdn_sample",
        grid=(Bs,),
        in_specs=[pl.BlockSpec((None, 3 * H, HEAD_DIM), lambda b: (b, 0, 0)),
                  pl.BlockSpec((None, 1, HEAD_DIM), lambda b: (b, 0, 0)), sspec],
        out_specs=[pl.BlockSpec((None, H, HEAD_DIM), lambda b: (b, 0, 0)), sspec],
        out_shape=[jax.ShapeDtypeStruct((Bs, H, HEAD_DIM), F32),
                   jax.ShapeDtypeStruct(state.shape, F32)],
        compiler_params=_params("arbitrary"),
    )(qkv3, gb3, state)


def _gated_norm_body(o_ref, z_ref, g_ref, out_ref):
    for h in range(o_ref.shape[1] // HEAD_DIM):
        c = slice(h * HEAD_DIM, (h + 1) * HEAD_DIM)
        out_ref[:, c] = (_rms(o_ref[:, c], g_ref[...]) * _silu(z_ref[:, c])).astype(out_ref.dtype)


def _gated_norm(o_g, proj, gdn_norm):
    T = o_g.shape[0]
    W = 8 * HEAD_DIM
    assert OFF_Z % W == 0 and V_GDN % W == 0
    tm = _pick(T, (640, 512, 256, 128))
    z_col0 = OFF_Z // W
    return pl.pallas_call(
        _gated_norm_body,
        name="gdn_gated_norm",
        grid=(T // tm, V_GDN // W),
        in_specs=[pl.BlockSpec((tm, W), lambda r, c: (r, c)),
                  pl.BlockSpec((tm, W), lambda r, c: (r, z_col0 + c)),
                  pl.BlockSpec((1, HEAD_DIM), lambda r, c: (0, 0))],
        out_specs=pl.BlockSpec((tm, W), lambda r, c: (r, c)),
        out_shape=jax.ShapeDtypeStruct((T, V_GDN), BF16),
        compiler_params=_params("arbitrary", "arbitrary"),
    )(o_g, proj, gdn_norm)


def _route_body(lg_ref, bias_ref, idx_ref, wt_ref):
    scores = jax.nn.sigmoid(lg_ref[...])
    biased = scores + bias_ref[...]
    shape = scores.shape
    E = shape[1]
    per_grp = E // N_GROUPS
    lane_i = lax.broadcasted_iota(jnp.int32, shape, 1)
    lane = lane_i.astype(F32)
    ninf = -jnp.inf

    def in_grp(g):
        return jnp.logical_and(lane_i >= g * per_grp, lane_i < (g + 1) * per_grp)

    gscore = []
    for g in range(N_GROUPS):
        xg = jnp.where(in_grp(g), biased, ninf)
        m1 = jnp.max(xg, axis=1, keepdims=True)
        i1 = jnp.min(jnp.where(xg == m1, lane, float(E)), axis=1, keepdims=True)
        m2 = jnp.max(jnp.where(lane == i1, ninf, xg), axis=1, keepdims=True)
        gscore.append(m1 + m2)
    keep = jnp.zeros(shape, F32)
    for g in range(N_GROUPS):
        rank = jnp.zeros_like(gscore[g])
        for o in range(N_GROUPS):
            if o != g:
                rank = rank + _beats(gscore[o], gscore[g], o < g)
        keep = jnp.where(in_grp(g), jnp.where(rank < TOPK_GROUPS, 1.0, 0.0), keep)

    cur = jnp.where(keep > 0.0, biased, ninf)
    idx = jnp.zeros(shape, F32)
    wts = jnp.zeros(shape, F32)
    wsum = jnp.zeros((shape[0], 1), F32)
    for k in range(TOP_K):
        mk = jnp.max(cur, axis=1, keepdims=True)
        ik = jnp.min(jnp.where(cur == mk, lane, float(E)), axis=1, keepdims=True)
        hit = lane == ik
        wk = jnp.sum(jnp.where(hit, scores, 0.0), axis=1, keepdims=True)
        cur = jnp.where(hit, ninf, cur)
        idx = jnp.where(lane_i == k, ik, idx)
        wts = jnp.where(lane_i == k, wk, wts)
        wsum = wsum + wk
    idx_ref[...] = idx.astype(jnp.int32)
    wt_ref[...] = wts / wsum * ROUTED_SCALE


def _route(logits, e_bias):
    T, E = logits.shape
    return pl.pallas_call(
        _route_body,
        name="route",
        grid=(T // ROW_BLK,),
        in_specs=[_row_spec(E), _vec_spec(E)],
        out_specs=[_row_spec(E), _row_spec(E)],
        out_shape=[jax.ShapeDtypeStruct((T, E), jnp.int32), jax.ShapeDtypeStruct((T, E), F32)],
        compiler_params=_params("arbitrary"),
    )(logits, e_bias)


def _ffn_body(x_ref, w1_ref, w3_ref, w2_ref, o_ref):
    x = x_ref[...]
    a = _silu(jnp.dot(x, w1_ref[...], preferred_element_type=F32))
    a = a * jnp.dot(x, w3_ref[...], preferred_element_type=F32)
    o_ref[...] = jnp.dot(a.astype(BF16), w2_ref[...], preferred_element_type=F32)


def _shared_ffn(x, w1, w3, w2):
    T, D = x.shape
    F = w1.shape[1]
    tm = _pick(T, (640, 512, 256, 128))
    return pl.pallas_call(
        _ffn_body,
        name="shared_ffn",
        grid=(T // tm,),
        in_specs=[pl.BlockSpec((tm, D), lambda r: (r, 0)),
                  pl.BlockSpec((D, F), lambda r: (0, 0)), pl.BlockSpec((D, F), lambda r: (0, 0)),
                  pl.BlockSpec((F, D), lambda r: (0, 0))],
        out_specs=pl.BlockSpec((tm, D), lambda r: (r, 0)),
        out_shape=jax.ShapeDtypeStruct((T, D), F32),
        compiler_params=_params("arbitrary"),
    )(x, w1, w3, w2)


def _grouped_ffn_body(be_ref, nu_ref, bv_ref, c0_ref, ta_ref, tb_ref, na_ref, nb_ref, hp_hbm,
                      w1_ref, w3_ref, w2_ref, o_ref, xbuf, sem, h1_ref, h3_ref, act_ref):
    del be_ref
    i = pl.program_id(0)
    p = pl.program_id(1)
    n_used = nu_ref[0]
    used = i < n_used
    n_valid = bv_ref[i]
    slot = lax.rem(i, 2)
    subs = [(s0, min(MOE_SUB, MOE_BLK - s0)) for s0 in range(0, MOE_BLK, MOE_SUB)]

    def row_copy(tok, slot_, r):
        return pltpu.make_async_copy(hp_hbm.at[tok], xbuf.at[slot_, :, pl.ds(r, 1), :], sem.at[slot_])

    def request(blk, slot_, ca_ref, cb_ref, r_lo, n_rows):
        off = lax.rem(c0_ref[blk], TOK_CHUNK)

        def body(t, carry):
            r = r_lo + t
            j = off + r
            tok = jnp.where(j < TOK_CHUNK, ca_ref[0, jnp.minimum(j, TOK_CHUNK - 1)],
                            cb_ref[0, jnp.maximum(j - TOK_CHUNK, 0)])
            row_copy(tok, slot_, r).start()
            return carry

        lax.fori_loop(0, n_rows, body, 0, unroll=ROW_UNROLL)

    @pl.when(jnp.logical_and(i == 0, p == 0))
    def _prime():
        request(0, 0, ta_ref, tb_ref, 0, MOE_BLK)

    @pl.when(i + 1 < n_used)
    def _prefetch():
        q = MOE_BLK // 3

        @pl.when(p < 3)
        def _part():
            request(i + 1, 1 - slot, na_ref, nb_ref, p * q, q)

    @pl.when(jnp.logical_and(used, p == 0))
    def _await_rows():
        def body(t, carry):
            row_copy(0, slot, 0).wait()
            return carry

        lax.fori_loop(0, MOE_BLK, body, 0, unroll=ROW_UNROLL)

    @pl.when(jnp.logical_and(used, p < 2))
    def _up():
        w1 = w1_ref[...].astype(BF16)
        w3 = w3_ref[...].astype(BF16)
        for s0, sz in subs:
            @pl.when(s0 < n_valid)
            def _sub(s0=s0, sz=sz):
                xu = jnp.concatenate([xbuf[slot, c, s0:s0 + sz, :] for c in range(xbuf.shape[1])], axis=1)
                xbits = jnp.where(p == 0, jnp.left_shift(xu, jnp.uint32(16)),
                                  jnp.bitwise_and(xu, jnp.uint32(0xFFFF0000)))
                x = pltpu.bitcast(xbits, F32).astype(BF16)
                a1 = jnp.dot(x, w1, preferred_element_type=F32)
                a3 = jnp.dot(x, w3, preferred_element_type=F32)

                @pl.when(p == 0)
                def _first():
                    h1_ref[s0:s0 + sz, :] = a1
                    h3_ref[s0:s0 + sz, :] = a3

                @pl.when(p == 1)
                def _second():
                    act_ref[s0:s0 + sz, :] = (_silu(h1_ref[s0:s0 + sz, :] + a1)
                                              * (h3_ref[s0:s0 + sz, :] + a3)).astype(BF16)

    @pl.when(jnp.logical_and(used, p >= 2))
    def _down():
        w2 = w2_ref[...].astype(BF16)
        for s0, sz in subs:
            @pl.when(s0 < n_valid)
            def _sub(s0=s0, sz=sz):
                o_ref[s0:s0 + sz, :] = jnp.dot(act_ref[s0:s0 + sz, :], w2, preferred_element_type=F32)

            @pl.when(s0 >= n_valid)
            def _pad(s0=s0, sz=sz):
                o_ref[s0:s0 + sz, :] = jnp.zeros((sz, o_ref.shape[1]), F32)

    @pl.when(jnp.logical_not(used))
    def _unused_block():
        o_ref[...] = jnp.zeros_like(o_ref)


def _grouped_ffn(h_packed, tok_chunks, c0, block_e, n_used, block_valid, w1, w3, w2):
    T, n_planes, _, _ = h_packed.shape
    Dh = n_planes * HEAD_DIM
    D = 2 * Dh
    E, _, F = w1.shape
    nb = block_e.shape[0]
    n_chunks = tok_chunks.shape[0]

    def last_used(i, nu):
        return jnp.minimum(i, nu[0] - 1)

    def k_half(i, p, nu):
        return jnp.where(i < nu[0], jnp.minimum(p, 1), 1)

    def n_half(i, p, nu):
        return jnp.where(i < nu[0], jnp.maximum(p - 2, 0), 1)

    def chunk(blk_of, plus):
        def index(i, p, be, nu, bv, c0_):
            q = c0_[last_used(blk_of(i), nu)] // TOK_CHUNK + plus
            return (jnp.minimum(q, n_chunks - 1), 0, 0)
        return pl.BlockSpec((None, 1, TOK_CHUNK), index, memory_space=pltpu.SMEM)

    return pl.pallas_call(
        _grouped_ffn_body,
        name="moe_grouped_ffn",
        grid_spec=pltpu.PrefetchScalarGridSpec(
            num_scalar_prefetch=4,
            grid=(nb, 4),
            in_specs=[
                chunk(lambda i: i, 0), chunk(lambda i: i, 1),
                chunk(lambda i: i + 1, 0), chunk(lambda i: i + 1, 1),
                pl.BlockSpec(memory_space=pl.ANY),
                pl.BlockSpec((None, Dh, F), lambda i, p, be, nu, bv, c0_: (be[i], k_half(i, p, nu), 0)),
                pl.BlockSpec((None, Dh, F), lambda i, p, be, nu, bv, c0_: (be[i], k_half(i, p, nu), 0)),
                pl.BlockSpec((None, F, Dh), lambda i, p, be, nu, bv, c0_: (be[i], 0, n_half(i, p, nu))),
            ],
            out_specs=pl.BlockSpec((MOE_BLK, Dh),
                                   lambda i, p, be, nu, bv, c0_: (jnp.minimum(i, nu[0]),
                                                                  jnp.where(i < nu[0], jnp.maximum(p - 2, 0), 0))),
            scratch_shapes=[pltpu.VMEM((2, n_planes, MOE_BLK, HEAD_DIM), jnp.uint32), pltpu.SemaphoreType.DMA((2,)),
                            pltpu.VMEM((MOE_BLK, F), F32), pltpu.VMEM((MOE_BLK, F), F32),
                            pltpu.VMEM((MOE_BLK, F), BF16)],
        ),
        out_shape=jax.ShapeDtypeStruct(((nb + 1) * MOE_BLK, D), F32),
        compiler_params=_params("arbitrary", "arbitrary"),
    )(block_e, n_used, block_valid, c0, tok_chunks, tok_chunks, tok_chunks, tok_chunks, h_packed, w1, w3, w2)


def _moe_routed(h_packed, idx, wts, w1, w3, w2):
    T = h_packed.shape[0]
    E = w1.shape[0]
    A = T * TOP_K
    nb = -(-A // MOE_BLK) + E
    i32 = jnp.int32
    flat_t = jnp.repeat(jnp.arange(T, dtype=i32), TOP_K)
    _, st = lax.sort((idx.reshape(-1), flat_t), num_keys=1, is_stable=True)
    e_iota = jnp.arange(E, dtype=i32)
    onehot = idx[:, :, None] == e_iota[None, None, :]
    csum = jnp.cumsum(onehot.any(axis=1).astype(i32), axis=0)
    counts = csum[-1]
    pcounts = (counts + MOE_BLK - 1) // MOE_BLK * MOE_BLK
    start = jnp.cumsum(counts) - counts
    pend = jnp.cumsum(pcounts)
    pstart = pend - pcounts
    n_used = (pend[-1:] // MOE_BLK).astype(i32)
    blk0 = jnp.arange(nb, dtype=i32) * MOE_BLK
    block_e = jnp.minimum(jnp.searchsorted(pend, blk0, side='right'), E - 1).astype(i32)
    block_e = block_e[jnp.minimum(jnp.arange(nb), n_used[0] - 1)]
    off_in_e = blk0 - pstart[block_e]
    block_valid = jnp.clip(counts[block_e] - off_in_e, 0, MOE_BLK).astype(i32)
    c0 = jnp.clip(start[block_e] + off_in_e, 0, A).astype(i32)
    n_chunks = -(-A // TOK_CHUNK) + 1
    tok_chunks = jnp.pad(st, (0, n_chunks * TOK_CHUNK - A)).reshape(n_chunks, 1, TOK_CHUNK)
    hp = h_packed.reshape(T, h_packed.shape[1] // HEAD_DIM, 1, HEAD_DIM)
    y_sorted = _grouped_ffn(hp, tok_chunks, c0, block_e, n_used, block_valid, w1, w3, w2)
    slot = (pstart[None, :] + csum - 1)[:, None, :]
    dest = jnp.sum(jnp.where(onehot, slot, 0), axis=-1)
    y_tok = jnp.take(y_sorted, dest, axis=0, mode="clip")
    return jnp.sum(y_tok * wts[:, :, None], axis=1)


def _rope_tables(pos):
    half = ROT_DIM // 2
    inv = jnp.power(ROPE_THETA, -jnp.arange(half, dtype=F32) / half)
    ang = pos.astype(F32)[:, None] * inv[None, :]
    cos, sin = jnp.cos(ang), jnp.sin(ang)
    n = pos.shape[0]
    ones = jnp.ones((n, HEAD_DIM - ROT_DIM), F32)
    zeros = jnp.zeros((n, HEAD_DIM - half), F32)
    cos_t = jnp.concatenate([cos, cos, ones], axis=1)
    sin_a = jnp.concatenate([-sin, zeros], axis=1)
    sin_b = jnp.concatenate([jnp.zeros((n, half), F32), sin, jnp.zeros((n, HEAD_DIM - ROT_DIM), F32)], axis=1)
    return cos_t, sin_a, sin_b


def kernel(x_prompt, x_sample, c_prompt, c_sample, cache_k, cache_v, state_ssm, state_conv, page_table, w_ada, b_ada, norm_mix_pre, norm_mix_post, norm_ffn_pre, norm_ffn_post, w_in, conv_w, a_log, dt_bias, gdn_norm, w_out, w_router, e_bias, w1, w3, w2, ws1, ws3, ws2):
    B, S, D = x_prompt.shape
    Bs = x_sample.shape[0]
    assert x_sample.shape[1] == 1 and w_ada.shape[0] == 1
    assert Bs == ROW_BLK and S % MOBA_BLOCK == 0 and S % ROW_BLK == 0
    Tp = B * S
    T = Tp + Bs
    rb_per_seq = S // ROW_BLK
    past_len = page_table.shape[1] * PAGE_SIZE
    H = GDN_HEADS

    x_all = jnp.concatenate([x_prompt.reshape(Tp, D), x_sample.reshape(Bs, D)], axis=0)

    c_all = jnp.concatenate([c_prompt, c_sample], axis=0)
    n_c = c_all.shape[0]
    c_pad = jnp.pad(c_all, ((0, (-n_c) % 8), (0, 0)))
    mod = _matmul(c_pad, w_ada[0], b_ada, name="adaln", silu_in=True, tm=c_pad.shape[0])
    mod_rows = jnp.concatenate([jnp.repeat(mod[:B], ROW_BLK, axis=0), mod[B:B + Bs]], axis=0)

    h = _prenorm(x_all, norm_mix_pre, mod_rows, 0, 1, rb_per_seq)
    w_in0 = w_in[0]
    proj = _matmul(h, w_in0, name="in_proj", n_cols=OFF_AB)
    w_ab = jnp.pad(w_in0[:, OFF_AB:], ((0, 0), (0, HEAD_DIM - 2 * H)))
    ab = _matmul(h, w_ab, name="in_proj_gates")

    pos = jnp.concatenate([jnp.tile(jnp.arange(S, dtype=jnp.int32), B),
                           jnp.full((Bs,), past_len, jnp.int32)])
    qk_rot = _rope(proj, *_rope_tables(pos))
    k_rot = qk_rot[:, Q_ATTN:]
    v_new = proj[:, OFF_V:OFF_V + KV_ATTN]
    o_a_p = _attn_prefill(qk_rot, proj, B, S)
    grp = ATTN_HEADS // KV_HEADS
    q_s = qk_rot[Tp:, :Q_ATTN].reshape(Bs, ATTN_HEADS, HEAD_DIM)
    kx = jnp.repeat(k_rot[Tp:].reshape(Bs, KV_HEADS, HEAD_DIM), grp, axis=1)
    vx = jnp.repeat(v_new[Tp:].reshape(Bs, KV_HEADS, HEAD_DIM), grp, axis=1)
    o_a_s = _attn_decode(q_s, kx, vx, cache_k.reshape(-1, HEAD_DIM), cache_v.reshape(-1, HEAD_DIM), page_table)
    o_a = jnp.concatenate([o_a_p, o_a_s.reshape(Bs, Q_ATTN)], axis=0)

    alog_pad = jnp.pad(a_log.astype(F32), ((0, 0), (0, HEAD_DIM - H)))
    dtb_pad = jnp.pad(dt_bias.astype(F32), ((0, 0), (0, HEAD_DIM - H)))
    gb = _gates(ab, alog_pad, dtb_pad)
    qkv_p = _conv_prompt(proj, conv_w[0], B, S)
    buf_t = state_conv[0].transpose(1, 0, 2)
    qkv_s = _conv_sample(proj, buf_t, conv_w[0], Tp // Bs)
    n_chunks = S // GDN_CHUNK

    def chunk_rows(a):
        return a.reshape(B, n_chunks, GDN_CHUNK, H).transpose(0, 1, 3, 2)

    o_g_p, ssm_p = _gdn_prompt(qkv_p, chunk_rows(gb[:Tp, :H]), chunk_rows(gb[:Tp, H:2 * H]), B, S)
    o_g_s, ssm_s = _gdn_sample(qkv_s.reshape(Bs, 3 * H, HEAD_DIM), gb[Tp:].reshape(Bs, 1, HEAD_DIM),
                               state_ssm[0].astype(F32))
    o_g = jnp.concatenate([o_g_p, o_g_s.reshape(Bs, V_GDN)], axis=0)
    o_gn = _gated_norm(o_g, proj, gdn_norm)

    mix = _matmul(jnp.concatenate([o_a, o_gn], axis=1), w_out[0], name="out_proj")

    x1, h2, h2_packed, logits = _mid(x_all, mix, norm_mix_post, norm_ffn_pre, mod_rows, w_router[0], rb_per_seq)
    idx_l, wts_l = _route(logits, e_bias)
    routed = _moe_routed(h2_packed, idx_l[:, :TOP_K], wts_l[:, :TOP_K], w1[0], w3[0], w2[0])
    shared = _shared_ffn(h2, ws1[0].astype(BF16), ws3[0].astype(BF16), ws2[0].astype(BF16))
    y_all = _final(x1, routed, shared, norm_ffn_post, mod_rows, rb_per_seq)

    y_prompt = y_all[:Tp].reshape(B, S, D)
    y_sample = y_all[Tp:].reshape(Bs, 1, D)
    k_prompt = k_rot[:Tp].reshape(1, B, S, KV_HEADS, HEAD_DIM)
    v_prompt = v_new[:Tp].reshape(1, B, S, KV_HEADS, HEAD_DIM)
    k_sample = k_rot[Tp:].reshape(1, Bs, 1, KV_HEADS, HEAD_DIM)
    v_sample = v_new[Tp:].reshape(1, Bs, 1, KV_HEADS, HEAD_DIM)
    def pre_conv_rows(r0, r1):
        return lax.slice(proj, (r0, OFF_CONV), (r1, OFF_CONV + GDN_CONV_CH))

    conv_prompt = jnp.stack([pre_conv_rows((b + 1) * S - (CONV_W - 1), (b + 1) * S) for b in range(B)])[None]
    conv_sample = jnp.concatenate([state_conv[0][:, 1:], pre_conv_rows(Tp, T)[:, None, :]], axis=1)[None]
    return (y_prompt, y_sample, k_prompt, v_prompt, k_sample, v_sample,
            ssm_p[None].astype(state_ssm.dtype), ssm_s[None].astype(state_ssm.dtype),
            conv_prompt.astype(state_conv.dtype), conv_sample.astype(state_conv.dtype))
```

```python
import functools

import jax
import jax.numpy as jnp
from jax import lax
from jax.experimental import pallas as pl
from jax.experimental.pallas import tpu as pltpu

F32 = jnp.float32
BF16 = jnp.bfloat16
HIGHEST = lax.Precision.HIGHEST

HEAD_DIM = 128
ATTN_HEADS = 16
KV_HEADS = 4
GDN_HEADS = 16
CONV_W = 4
GDN_CHUNK = 64
MOBA_BLOCK = 256
MOBA_TOPK = 3
ROPE_THETA = 500000.0
ROT_DIM = HEAD_DIM // 4
PAGE_SIZE = 128
N_GROUPS = 8
TOPK_GROUPS = 4
TOP_K = 8
ROUTED_SCALE = 2.5
NORM_EPS = 1e-6

Q_ATTN = ATTN_HEADS * HEAD_DIM
KV_ATTN = KV_HEADS * HEAD_DIM
QK_GDN = GDN_HEADS * HEAD_DIM
V_GDN = GDN_HEADS * HEAD_DIM
GDN_CONV_CH = 2 * QK_GDN + V_GDN
OFF_K = Q_ATTN
OFF_V = Q_ATTN + KV_ATTN
OFF_CONV = Q_ATTN + 2 * KV_ATTN
OFF_Z = OFF_CONV + GDN_CONV_CH
OFF_AB = OFF_Z + V_GDN

ROW_BLK = 128
MOE_BLK = 576
MOE_SUB = 288
TOK_CHUNK = 1024
ROW_UNROLL = 8
NEG = -1e30
VMEM_LIMIT = 56 << 20


def _params(*sem):
    return pltpu.CompilerParams(dimension_semantics=sem, vmem_limit_bytes=VMEM_LIMIT)


def _pick(n, prefs):
    for p in prefs:
        if n % p == 0:
            return p
    return n


def _nt(a, b, **kw):
    return lax.dot_general(a, b, (((1,), (1,)), ((), ())), preferred_element_type=F32, **kw)


def _silu(x):
    return x * jax.nn.sigmoid(x)


def _rms(x, g):
    return x * lax.rsqrt(jnp.mean(x * x, axis=-1, keepdims=True) + NORM_EPS) * g


def _beats(a, b, a_first):
    return jnp.where(a > b, 1.0, jnp.where(a == b, jnp.where(a_first, 1.0, 0.0), 0.0))


def _mm_body(x_ref, w_ref, b_ref, o_ref, wbf_ref, *, silu_in):
    @pl.when(pl.program_id(1) == 0)
    def _cast_weight_tile():
        wbf_ref[...] = w_ref[...].astype(BF16)

    x = x_ref[...]
    if silu_in:
        x = _silu(x.astype(F32))
    acc = jnp.dot(x.astype(BF16), wbf_ref[...], preferred_element_type=F32)
    o_ref[...] = (acc + b_ref[...]).astype(o_ref.dtype)


def _matmul(x, w, b=None, *, name, n_cols=None, silu_in=False, out_dtype=F32, tm=None, tn=None):
    M, K = x.shape
    N = n_cols or w.shape[1]
    tm = tm or _pick(M, (640, 512, 256, 128))
    tn = tn or _pick(N, (512, 256, 128))
    if b is None:
        b = jnp.zeros((1, N), F32)
    return pl.pallas_call(
        functools.partial(_mm_body, silu_in=silu_in),
        name=name,
        grid=(N // tn, M // tm),
        in_specs=[
            pl.BlockSpec((tm, K), lambda n, m: (m, 0)),
            pl.BlockSpec((K, tn), lambda n, m: (0, n)),
            pl.BlockSpec((1, tn), lambda n, m: (0, n)),
        ],
        out_specs=pl.BlockSpec((tm, tn), lambda n, m: (m, n)),
        out_shape=jax.ShapeDtypeStruct((M, N), out_dtype),
        scratch_shapes=[pltpu.VMEM((K, tn), BF16)],
        compiler_params=_params("arbitrary", "arbitrary"),
    )(x, w, b)


def _prenorm_body(x_ref, g_ref, sh_ref, sc_ref, o_ref):
    y = _rms(x_ref[...], g_ref[...]) * (1.0 + sc_ref[...]) + sh_ref[...]
    o_ref[...] = y.astype(o_ref.dtype)


def _mod_spec(D, slot, rb_per_seq):
    return pl.BlockSpec((ROW_BLK, D), lambda r: (r // rb_per_seq, slot))


def _row_spec(D):
    return pl.BlockSpec((ROW_BLK, D), lambda r: (r, 0))


def _vec_spec(D):
    return pl.BlockSpec((1, D), lambda r: (0, 0))


def _prenorm(x, g, mod, slot_sh, slot_sc, rb_per_seq):
    T, D = x.shape
    return pl.pallas_call(
        _prenorm_body,
        name="prenorm",
        grid=(T // ROW_BLK,),
        in_specs=[_row_spec(D), _vec_spec(D), _mod_spec(D, slot_sh, rb_per_seq),
                  _mod_spec(D, slot_sc, rb_per_seq)],
        out_specs=_row_spec(D),
        out_shape=jax.ShapeDtypeStruct((T, D), BF16),
        compiler_params=_params("arbitrary"),
    )(x, g, mod, mod)


def _mid_body(x_ref, mix_ref, gpost_ref, gt_ref, gpre_ref, sh_ref, sc_ref, wr_ref,
              x1_ref, h_ref, hp_ref, lg_ref):
    x1 = x_ref[...] + gt_ref[...] * _rms(mix_ref[...], gpost_ref[...])
    x1_ref[...] = x1
    h = _rms(x1, gpre_ref[...]) * (1.0 + sc_ref[...]) + sh_ref[...]
    hb = h.astype(BF16)
    h_ref[...] = hb
    bits = pltpu.bitcast(hb.astype(F32), jnp.uint32)
    dh = bits.shape[1] // 2
    hp_ref[...] = jnp.bitwise_or(jnp.bitwise_and(bits[:, dh:], jnp.uint32(0xFFFF0000)),
                                 jnp.right_shift(bits[:, :dh], jnp.uint32(16)))
    lg_ref[...] = jnp.dot(h, wr_ref[...], precision=HIGHEST, preferred_element_type=F32)


def _mid(x, mix, g_post, g_pre, mod, w_router, rb_per_seq):
    T, D = x.shape
    E = w_router.shape[1]
    return pl.pallas_call(
        _mid_body,
        name="mid_norms_router",
        grid=(T // ROW_BLK,),
        in_specs=[_row_spec(D), _row_spec(D), _vec_spec(D), _mod_spec(D, 2, rb_per_seq),
                  _vec_spec(D), _mod_spec(D, 3, rb_per_seq), _mod_spec(D, 4, rb_per_seq),
                  pl.BlockSpec((D, E), lambda r: (0, 0))],
        out_specs=[_row_spec(D), _row_spec(D), _row_spec(D // 2), _row_spec(E)],
        out_shape=[jax.ShapeDtypeStruct((T, D), F32), jax.ShapeDtypeStruct((T, D), BF16),
                   jax.ShapeDtypeStruct((T, D // 2), jnp.uint32), jax.ShapeDtypeStruct((T, E), F32)],
        compiler_params=_params("arbitrary"),
    )(x, mix, g_post, mod, g_pre, mod, mod, w_router)


def _final_body(x1_ref, a_ref, b_ref, g_ref, gt_ref, o_ref):
    f = a_ref[...] + b_ref[...]
    o_ref[...] = x1_ref[...] + gt_ref[...] * _rms(f, g_ref[...])


def _final(x1, routed, shared, g_post, mod, rb_per_seq):
    T, D = x1.shape
    return pl.pallas_call(
        _final_body,
        name="final_norm",
        grid=(T // ROW_BLK,),
        in_specs=[_row_spec(D), _row_spec(D), _row_spec(D), _vec_spec(D),
                  _mod_spec(D, 5, rb_per_seq)],
        out_specs=_row_spec(D),
        out_shape=jax.ShapeDtypeStruct((T, D), F32),
        compiler_params=_params("arbitrary"),
    )(x1, routed, shared, g_post, mod)


def _rope_body(x_ref, c_ref, sa_ref, sb_ref, o_ref):
    x = x_ref[...]
    half = ROT_DIM // 2
    o_ref[...] = (x * c_ref[...] + pltpu.roll(x, HEAD_DIM - half, 1) * sa_ref[...]
                  + pltpu.roll(x, half, 1) * sb_ref[...])


def _rope(proj, cos_t, sin_a, sin_b):
    T = proj.shape[0]
    n_heads = ATTN_HEADS + KV_HEADS
    tr = _pick(T, (640, 512, 256, 128))
    tab = pl.BlockSpec((tr, HEAD_DIM), lambda r, h: (r, 0))
    blk = pl.BlockSpec((tr, HEAD_DIM), lambda r, h: (r, h))
    return pl.pallas_call(
        _rope_body,
        name="rope",
        grid=(T // tr, n_heads),
        in_specs=[blk, tab, tab, tab],
        out_specs=blk,
        out_shape=jax.ShapeDtypeStruct((T, n_heads * HEAD_DIM), F32),
        compiler_params=_params("arbitrary", "arbitrary"),
    )(proj, cos_t, sin_a, sin_b)


def _topk_mask(gate, lane, n_valid, n_cand):
    gate = jnp.where(lane < n_valid, gate, -jnp.inf)
    rank = jnp.zeros(gate.shape, F32)
    for m in range(n_cand):
        rank = rank + _beats(gate[:, m:m + 1], gate, m < lane)
    return jnp.where(lane < n_valid, jnp.where(rank < MOBA_TOPK, 1.0, 0.0), 0.0)


def _attn_prefill_body(q_ref, k_ref, v_ref, o_ref, s_ref, mx_ref, ls_ref, acc_ref, sel_ref, *, nblk, grp):
    i = pl.program_id(2)
    blk = MOBA_BLOCK
    rows = grp * blk
    scale = HEAD_DIM ** -0.5
    qf = jnp.concatenate([q_ref[:, h * HEAD_DIM:(h + 1) * HEAD_DIM] for h in range(grp)], axis=0)
    qb = qf.astype(BF16)

    krow = lax.broadcasted_iota(jnp.int32, (HEAD_DIM, HEAD_DIM), 0)
    kmean = jnp.zeros((HEAD_DIM, HEAD_DIM), F32)
    for n in range(nblk):
        kmean = jnp.where(krow == n, jnp.mean(k_ref[n * blk:(n + 1) * blk, :], axis=0, keepdims=True), kmean)
    gate = _nt(qf, kmean, precision=HIGHEST)
    lane = lax.broadcasted_iota(jnp.int32, (rows, HEAD_DIM), 1)
    sel_ref[...] = _topk_mask(gate, lane, i, nblk)

    start = pl.multiple_of(i * blk, blk)
    s = _nt(qb, k_ref[pl.ds(start, blk), :].astype(BF16)) * scale
    qpos = jnp.bitwise_and(lax.broadcasted_iota(jnp.int32, (rows, blk), 0), blk - 1)
    kpos = lax.broadcasted_iota(jnp.int32, (rows, blk), 1)
    s = jnp.where(kpos <= qpos, s, NEG)
    s_ref[nblk - 1] = s
    mx_ref[...] = s

    for n in range(nblk - 1):
        @pl.when(n < i)
        def _scores(n=n):
            s = _nt(qb, k_ref[n * blk:(n + 1) * blk, :].astype(BF16)) * scale
            s = jnp.where(sel_ref[:, n:n + 1] > 0.0, s, NEG)
            s_ref[n] = s
            mx_ref[...] = jnp.maximum(mx_ref[...], s)

    m = jnp.max(mx_ref[...], axis=1, keepdims=True)
    p = jnp.exp(s_ref[nblk - 1] - m)
    ls_ref[...] = p
    acc_ref[...] = jnp.dot(p.astype(BF16), v_ref[pl.ds(start, blk), :].astype(BF16),
                           preferred_element_type=F32)

    for n in range(nblk - 1):
        @pl.when(n < i)
        def _weights(n=n):
            p = jnp.exp(s_ref[n] - m)
            ls_ref[...] = ls_ref[...] + p
            acc_ref[...] = acc_ref[...] + jnp.dot(
                p.astype(BF16), v_ref[n * blk:(n + 1) * blk, :].astype(BF16),
                preferred_element_type=F32)

    out = acc_ref[...] / jnp.sum(ls_ref[...], axis=1, keepdims=True)
    for h in range(grp):
        o_ref[:, h * HEAD_DIM:(h + 1) * HEAD_DIM] = out[h * blk:(h + 1) * blk].astype(o_ref.dtype)


def _attn_prefill(qk_rot, proj, B, S):
    nblk = S // MOBA_BLOCK
    grp = ATTN_HEADS // KV_HEADS
    rows = grp * MOBA_BLOCK
    k_col0 = OFF_K // HEAD_DIM
    v_col0 = OFF_V // HEAD_DIM
    return pl.pallas_call(
        functools.partial(_attn_prefill_body, nblk=nblk, grp=grp),
        name="attn_prefill",
        grid=(B, KV_HEADS, nblk),
        in_specs=[
            pl.BlockSpec((MOBA_BLOCK, grp * HEAD_DIM), lambda b, g, i: (b * nblk + i, g)),
            pl.BlockSpec((S, HEAD_DIM), lambda b, g, i: (b, k_col0 + g)),
            pl.BlockSpec((S, HEAD_DIM), lambda b, g, i: (b, v_col0 + g)),
        ],
        out_specs=pl.BlockSpec((MOBA_BLOCK, grp * HEAD_DIM), lambda b, g, i: (b * nblk + i, g)),
        out_shape=jax.ShapeDtypeStruct((B * S, Q_ATTN), BF16),
        scratch_shapes=[pltpu.VMEM((nblk, rows, MOBA_BLOCK), F32), pltpu.VMEM((rows, MOBA_BLOCK), F32),
                        pltpu.VMEM((rows, MOBA_BLOCK), F32), pltpu.VMEM((rows, HEAD_DIM), F32),
                        pltpu.VMEM((rows, HEAD_DIM), F32)],
        compiler_params=_params("arbitrary", "arbitrary", "arbitrary"),
    )(qk_rot, qk_rot, proj)


def _attn_decode_body(pt_ref, q_ref, kx_ref, vx_ref, *refs, n_pages):
    del pt_ref
    k_pages, v_pages, o_ref = refs[:n_pages], refs[n_pages:2 * n_pages], refs[2 * n_pages]
    H = ATTN_HEADS
    grp = ATTN_HEADS // KV_HEADS
    ppb = MOBA_BLOCK // PAGE_SIZE
    nblk = n_pages // ppb
    page_rows = PAGE_SIZE * KV_HEADS
    blk_cols = MOBA_BLOCK * KV_HEADS
    ncol = n_pages * page_rows
    scale = HEAD_DIM ** -0.5
    q = q_ref[...]
    row1 = lax.broadcasted_iota(jnp.int32, (H, 1), 0)
    lane = lax.broadcasted_iota(jnp.int32, (H, HEAD_DIM), 1)

    gate = jnp.zeros((H, HEAD_DIM), F32)
    for n in range(nblk):
        ks8 = k_pages[n * ppb][...].reshape(page_rows // 8, 8, HEAD_DIM).sum(axis=0)
        for j in range(1, ppb):
            ks8 = ks8 + k_pages[n * ppb + j][...].reshape(page_rows // 8, 8, HEAD_DIM).sum(axis=0)
        ksum = ks8[:KV_HEADS] + ks8[KV_HEADS:]
        kexp = jnp.zeros((H, HEAD_DIM), F32)
        for h in range(KV_HEADS):
            mine_h = jnp.logical_and(row1 >= h * grp, row1 < (h + 1) * grp)
            kexp = jnp.where(mine_h, ksum[h:h + 1, :], kexp)
        g_n = jnp.sum(q * kexp, axis=1, keepdims=True) / MOBA_BLOCK
        gate = jnp.where(lane == n, g_n, gate)
    sel = _topk_mask(gate, lane, nblk, nblk)

    kall = jnp.concatenate([kp[...].astype(BF16) for kp in k_pages], axis=0)
    s = _nt(q.astype(BF16), kall) * scale
    col = lax.broadcasted_iota(jnp.int32, (H, ncol), 1)
    row = lax.broadcasted_iota(jnp.int32, (H, ncol), 0)
    own_head = jnp.bitwise_and(col, KV_HEADS - 1) * grp == jnp.bitwise_and(row, -grp)
    picked = jnp.concatenate(
        [jnp.broadcast_to(sel[:, n:n + 1], (H, blk_cols)) for n in range(nblk)], axis=1)
    s = jnp.where(jnp.logical_and(own_head, picked > 0.0), s, NEG)
    s_own = jnp.sum(q * kx_ref[...], axis=1, keepdims=True) * scale
    m = jnp.maximum(jnp.max(s, axis=1, keepdims=True), s_own)
    p = jnp.exp(s - m)
    e_own = jnp.exp(s_own - m)
    denom = jnp.sum(p, axis=1, keepdims=True) + e_own
    vall = jnp.concatenate([vp[...].astype(BF16) for vp in v_pages], axis=0)
    out = (jnp.dot(p.astype(BF16), vall, preferred_element_type=F32) + e_own * vx_ref[...]) / denom
    o_ref[...] = out.astype(o_ref.dtype)


def _attn_decode(q_s, kx, vx, cache_k, cache_v, page_table):
    Bs, n_pages = page_table.shape
    assert n_pages % (MOBA_BLOCK // PAGE_SIZE) == 0
    assert KV_HEADS & (KV_HEADS - 1) == 0 and (ATTN_HEADS // KV_HEADS) & (ATTN_HEADS // KV_HEADS - 1) == 0
    vec = pl.BlockSpec((None, ATTN_HEADS, HEAD_DIM), lambda b, pt: (b, 0, 0))

    def page(j):
        return pl.BlockSpec((PAGE_SIZE * KV_HEADS, HEAD_DIM), lambda b, pt: (pt[b, j], 0))

    pages = [page(j) for j in range(n_pages)]
    return pl.pallas_call(
        functools.partial(_attn_decode_body, n_pages=n_pages),
        name="attn_decode",
        grid_spec=pltpu.PrefetchScalarGridSpec(
            num_scalar_prefetch=1,
            grid=(Bs,),
            in_specs=[vec, vec, vec] + pages + pages,
            out_specs=vec,
        ),
        out_shape=jax.ShapeDtypeStruct((Bs, ATTN_HEADS, HEAD_DIM), BF16),
        compiler_params=_params("arbitrary"),
    )(page_table, q_s, kx, vx, *([cache_k] * n_pages), *([cache_v] * n_pages))


def _conv_finish(y, c):
    y = _silu(y)
    nrm = y * lax.rsqrt(jnp.sum(y * y, axis=1, keepdims=True) + 1e-6)
    return jnp.where(c < GDN_HEADS, nrm * (HEAD_DIM ** -0.5), jnp.where(c < 2 * GDN_HEADS, nrm, y))


def _conv_prompt_body(x_ref, w_ref, o_ref):
    x = x_ref[...]
    row = lax.broadcasted_iota(jnp.int32, x.shape, 0)
    y = x * w_ref[CONV_W - 1:CONV_W, :]
    for k in range(1, CONV_W):
        y = y + jnp.where(row >= k, pltpu.roll(x, k, 0), 0.0) * w_ref[CONV_W - 1 - k:CONV_W - k, :]
    o_ref[...] = _conv_finish(y, pl.program_id(1))


def _conv_prompt(proj, conv_w, B, S):
    n_tiles = GDN_CONV_CH // HEAD_DIM
    col0 = OFF_CONV // HEAD_DIM
    return pl.pallas_call(
        _conv_prompt_body,
        name="conv_prompt",
        grid=(B, n_tiles),
        in_specs=[pl.BlockSpec((S, HEAD_DIM), lambda b, c: (b, col0 + c)),
                  pl.BlockSpec((CONV_W, HEAD_DIM), lambda b, c: (0, c))],
        out_specs=pl.BlockSpec((S, HEAD_DIM), lambda b, c: (b, c)),
        out_shape=jax.ShapeDtypeStruct((B * S, GDN_CONV_CH), F32),
        compiler_params=_params("arbitrary", "arbitrary"),
    )(proj, conv_w)


def _conv_sample_body(x_ref, buf_ref, w_ref, o_ref):
    y = x_ref[...] * w_ref[CONV_W - 1:CONV_W, :]
    for i in range(CONV_W - 1):
        y = y + buf_ref[i] * w_ref[i:i + 1, :]
    o_ref[...] = _conv_finish(y, pl.program_id(0))


def _conv_sample(proj, buf_t, conv_w, row_blk0):
    Bs = buf_t.shape[1]
    n_tiles = GDN_CONV_CH // HEAD_DIM
    col0 = OFF_CONV // HEAD_DIM
    return pl.pallas_call(
        _conv_sample_body,
        name="conv_sample",
        grid=(n_tiles,),
        in_specs=[pl.BlockSpec((Bs, HEAD_DIM), lambda c: (row_blk0, col0 + c)),
                  pl.BlockSpec((CONV_W - 1, Bs, HEAD_DIM), lambda c: (0, 0, c)),
                  pl.BlockSpec((CONV_W, HEAD_DIM), lambda c: (0, c))],
        out_specs=pl.BlockSpec((Bs, HEAD_DIM), lambda c: (0, c)),
        out_shape=jax.ShapeDtypeStruct((Bs, GDN_CONV_CH), F32),
        compiler_params=_params("arbitrary"),
    )(proj, buf_t, conv_w)


def _gates_body(ab_ref, alog_ref, dtb_ref, o_ref):
    x = ab_ref[...]
    sp_in = x + dtb_ref[...]
    softplus = jnp.maximum(sp_in, 0.0) + jnp.log1p(jnp.exp(-jnp.abs(sp_in)))
    g = -jnp.exp(alog_ref[...]) * softplus
    lane = lax.broadcasted_iota(jnp.int32, x.shape, 1)
    o_ref[...] = jnp.where(lane < GDN_HEADS, g, jax.nn.sigmoid(x))


def _gates(ab, alog_pad, dtb_pad):
    T = ab.shape[0]
    return pl.pallas_call(
        _gates_body,
        name="gdn_gates",
        grid=(T // ROW_BLK,),
        in_specs=[_row_spec(HEAD_DIM), _vec_spec(HEAD_DIM), _vec_spec(HEAD_DIM)],
        out_specs=_row_spec(HEAD_DIM),
        out_shape=jax.ShapeDtypeStruct((T, HEAD_DIM), F32),
        compiler_params=_params("arbitrary"),
    )(ab, alog_pad, dtb_pad)


def _mmb(a, b):
    return jnp.dot(a.astype(BF16), b.astype(BF16), preferred_element_type=F32)


def _split_bf16(a):
    hi = a.astype(BF16)
    return hi, (a - hi.astype(F32)).astype(BF16)


def _mm3(a, b):
    ah, al = _split_bf16(a)
    bh, bl = _split_bf16(b)

    def d(x, y):
        return jnp.dot(x, y, preferred_element_type=F32)

    return d(ah, bh) + (d(ah, bl) + d(al, bh))


def _unit_lower_inverse(lows):
    C = lows[0].shape[0]
    ii = lax.broadcasted_iota(jnp.int32, (C, C), 0)
    jj = lax.broadcasted_iota(jnp.int32, (C, C), 1)
    eye = jnp.where(ii == jj, 1.0, 0.0)
    invs = [eye - low for low in lows]
    pws = [_mm3(low, low) for low in lows]
    span = 2
    while span < C:
        invs = [inv + _mm3(inv, pw) for inv, pw in zip(invs, pws)]
        span *= 2
        if span < C:
            pws = [_mm3(pw, pw) for pw in pws]
    return invs


GDN_HEAD_GROUP = 8


def _gdn_chunk_body(q_ref, k_ref, v_ref, g_ref, b_ref, o_ref, s_out_ref, s_ref):
    n = pl.program_id(1)
    C = GDN_CHUNK

    @pl.when(n == 0)
    def _zero_state():
        s_ref[...] = jnp.zeros_like(s_ref)

    ii = lax.broadcasted_iota(jnp.int32, (C, C), 0)
    jj = lax.broadcasted_iota(jnp.int32, (C, C), 1)
    eye = ii == jj
    tril = jj <= ii
    for h0 in range(0, GDN_HEADS, GDN_HEAD_GROUP):
        hs = range(h0, h0 + GDN_HEAD_GROUP)
        cols = [slice(h * HEAD_DIM, (h + 1) * HEAD_DIM) for h in hs]
        q = [q_ref[:, c] for c in cols]
        k = [k_ref[:, c] for c in cols]
        v = [v_ref[:, c] for c in cols]
        gc_col = [jnp.sum(jnp.where(tril, g_ref[h:h + 1, :], 0.0), axis=1, keepdims=True) for h in hs]
        gc_row = [jnp.sum(jnp.where(eye, x, 0.0), axis=0, keepdims=True) for x in gc_col]
        beta = [jnp.sum(jnp.where(eye, b_ref[h:h + 1, :], 0.0), axis=1, keepdims=True) for h in hs]
        decay = [jnp.exp(jnp.where(tril, c_ - r_, -jnp.inf)) for c_, r_ in zip(gc_col, gc_row)]
        g_last = [x[C - 1:C, :] for x in gc_col]
        e_gc = [jnp.exp(x) for x in gc_col]
        kbeta = [k_ * b_ for k_, b_ in zip(k, beta)]
        k16 = [x.astype(BF16) for x in k]
        low = [jnp.where(ii > jj, _nt(kb.astype(BF16), kk) * d, 0.0) for kb, kk, d in zip(kbeta, k16, decay)]
        tmat = [t.astype(BF16) for t in _unit_lower_inverse(low)]
        u = [_mmb(t, v_ * b_) for t, v_, b_ in zip(tmat, v, beta)]
        w = [_mmb(t, kb * e) for t, kb, e in zip(tmat, kbeta, e_gc)]
        attn = [_nt(q_.astype(BF16), kk) * d for q_, kk, d in zip(q, k16, decay)]
        state = [s_ref[h] for h in hs]
        v_new = [u_ - _mmb(w_, s_) for u_, w_, s_ in zip(u, w, state)]
        for c, q_, e, s_, a_, vn in zip(cols, q, e_gc, state, attn, v_new):
            o_ref[:, c] = _mmb(q_ * e, s_) + _mmb(a_, vn)
        for h, k_, gl, gc, s_, vn in zip(hs, k, g_last, gc_col, state, v_new):
            k_tail = k_ * jnp.exp(gl - gc)
            s_ref[h] = s_ * jnp.exp(gl) + _mmb(k_tail.T, vn)

    @pl.when(n == pl.num_programs(1) - 1)
    def _emit_state():
        s_out_ref[...] = s_ref[...]


def _gdn_prompt(qkv, g_rows, b_rows, B, S):
    N = S // GDN_CHUNK
    W = GDN_HEADS * HEAD_DIM

    def part(j):
        return pl.BlockSpec((GDN_CHUNK, W), lambda b, n: (b * N + n, j))

    gspec = pl.BlockSpec((None, None, GDN_HEADS, GDN_CHUNK), lambda b, n: (b, n, 0, 0))
    return pl.pallas_call(
        _gdn_chunk_body,
        name="gdn_prompt",
        grid=(B, N),
        in_specs=[part(0), part(1), part(2), gspec, gspec],
        out_specs=[pl.BlockSpec((GDN_CHUNK, W), lambda b, n: (b * N + n, 0)),
                   pl.BlockSpec((None, GDN_HEADS, HEAD_DIM, HEAD_DIM), lambda b, n: (b, 0, 0, 0))],
        out_shape=[jax.ShapeDtypeStruct((B * S, W), F32),
                   jax.ShapeDtypeStruct((B, GDN_HEADS, HEAD_DIM, HEAD_DIM), F32)],
        scratch_shapes=[pltpu.VMEM((GDN_HEADS, HEAD_DIM, HEAD_DIM), F32)],
        compiler_params=_params("arbitrary", "arbitrary"),
    )(qkv, qkv, qkv, g_rows, b_rows)


def _gdn_step_body(qkv_ref, gb_ref, s_ref, o_ref, s_out_ref):
    H = GDN_HEADS
    ii = lax.broadcasted_iota(jnp.int32, (HEAD_DIM, HEAD_DIM), 0)
    jj = lax.broadcasted_iota(jnp.int32, (HEAD_DIM, HEAD_DIM), 1)
    eye = ii == jj
    lane = lax.broadcasted_iota(jnp.int32, (1, HEAD_DIM), 1)
    gb = gb_ref[...]
    for h in range(H):
        q_col = jnp.sum(jnp.where(eye, qkv_ref[h:h + 1, :], 0.0), axis=1, keepdims=True)
        k_col = jnp.sum(jnp.where(eye, qkv_ref[H + h:H + h + 1, :], 0.0), axis=1, keepdims=True)
        v_row = qkv_ref[2 * H + h:2 * H + h + 1, :]
        g = jnp.sum(jnp.where(lane == h, gb, 0.0), axis=1, keepdims=True)
        beta = jnp.sum(jnp.where(lane == H + h, gb, 0.0), axis=1, keepdims=True)
        state = s_ref[h] * jnp.exp(g)
        kv = jnp.sum(k_col * state, axis=0, keepdims=True)
        delta = (v_row - kv) * beta
        state = state + k_col * delta
        s_out_ref[h] = state
        o_ref[h:h + 1, :] = jnp.sum(q_col * state, axis=0, keepdims=True)


def _gdn_sample(qkv3, gb3, state):
    Bs = qkv3.shape[0]
    H = GDN_HEADS
    sspec = pl.BlockSpec((None, H, HEAD_DIM, HEAD_DIM), lambda b: (b, 0, 0, 0))
    return pl.pallas_call(
        _gdn_step_body,
        name="gdn_sample",
        grid=(Bs,),
        in_specs=[pl.BlockSpec((None, 3 * H, HEAD_DIM), lambda b: (b, 0, 0)),
                  pl.BlockSpec((None, 1, HEAD_DIM), lambda b: (b, 0, 0)), sspec],
        out_specs=[pl.BlockSpec((None, H, HEAD_DIM), lambda b: (b, 0, 0)), sspec],
        out_shape=[jax.ShapeDtypeStruct((Bs, H, HEAD_DIM), F32),
                   jax.ShapeDtypeStruct(state.shape, F32)],
        compiler_params=_params("arbitrary"),
    )(qkv3, gb3, state)


def _gated_norm_body(o_ref, z_ref, g_ref, out_ref):
    for h in range(o_ref.shape[1] // HEAD_DIM):
        c = slice(h * HEAD_DIM, (h + 1) * HEAD_DIM)
        out_ref[:, c] = (_rms(o_ref[:, c], g_ref[...]) * _silu(z_ref[:, c])).astype(out_ref.dtype)


def _gated_norm(o_g, proj, gdn_norm):
    T = o_g.shape[0]
    W = 8 * HEAD_DIM
    assert OFF_Z % W == 0 and V_GDN % W == 0
    tm = _pick(T, (640, 512, 256, 128))
    z_col0 = OFF_Z // W
    return pl.pallas_call(
        _gated_norm_body,
        name="gdn_gated_norm",
        grid=(T // tm, V_GDN // W),
        in_specs=[pl.BlockSpec((tm, W), lambda r, c: (r, c)),
                  pl.BlockSpec((tm, W), lambda r, c: (r, z_col0 + c)),
                  pl.BlockSpec((1, HEAD_DIM), lambda r, c: (0, 0))],
        out_specs=pl.BlockSpec((tm, W), lambda r, c: (r, c)),
        out_shape=jax.ShapeDtypeStruct((T, V_GDN), BF16),
        compiler_params=_params("arbitrary", "arbitrary"),
    )(o_g, proj, gdn_norm)


def _route_body(lg_ref, bias_ref, idx_ref, wt_ref):
    scores = jax.nn.sigmoid(lg_ref[...])
    biased = scores + bias_ref[...]
    shape = scores.shape
    E = shape[1]
    per_grp = E // N_GROUPS
    lane_i = lax.broadcasted_iota(jnp.int32, shape, 1)
    lane = lane_i.astype(F32)
    ninf = -jnp.inf

    def in_grp(g):
        return jnp.logical_and(lane_i >= g * per_grp, lane_i < (g + 1) * per_grp)

    gscore = []
    for g in range(N_GROUPS):
        xg = jnp.where(in_grp(g), biased, ninf)
        m1 = jnp.max(xg, axis=1, keepdims=True)
        i1 = jnp.min(jnp.where(xg == m1, lane, float(E)), axis=1, keepdims=True)
        m2 = jnp.max(jnp.where(lane == i1, ninf, xg), axis=1, keepdims=True)
        gscore.append(m1 + m2)
    keep = jnp.zeros(shape, F32)
    for g in range(N_GROUPS):
        rank = jnp.zeros_like(gscore[g])
        for o in range(N_GROUPS):
            if o != g:
                rank = rank + _beats(gscore[o], gscore[g], o < g)
        keep = jnp.where(in_grp(g), jnp.where(rank < TOPK_GROUPS, 1.0, 0.0), keep)

    cur = jnp.where(keep > 0.0, biased, ninf)
    idx = jnp.zeros(shape, F32)
    wts = jnp.zeros(shape, F32)
    wsum = jnp.zeros((shape[0], 1), F32)
    for k in range(TOP_K):
        mk = jnp.max(cur, axis=1, keepdims=True)
        ik = jnp.min(jnp.where(cur == mk, lane, float(E)), axis=1, keepdims=True)
        hit = lane == ik
        wk = jnp.sum(jnp.where(hit, scores, 0.0), axis=1, keepdims=True)
        cur = jnp.where(hit, ninf, cur)
        idx = jnp.where(lane_i == k, ik, idx)
        wts = jnp.where(lane_i == k, wk, wts)
        wsum = wsum + wk
    idx_ref[...] = idx.astype(jnp.int32)
    wt_ref[...] = wts / wsum * ROUTED_SCALE


def _route(logits, e_bias):
    T, E = logits.shape
    return pl.pallas_call(
        _route_body,
        name="route",
        grid=(T // ROW_BLK,),
        in_specs=[_row_spec(E), _vec_spec(E)],
        out_specs=[_row_spec(E), _row_spec(E)],
        out_shape=[jax.ShapeDtypeStruct((T, E), jnp.int32), jax.ShapeDtypeStruct((T, E), F32)],
        compiler_params=_params("arbitrary"),
    )(logits, e_bias)


def _ffn_body(x_ref, w1_ref, w3_ref, w2_ref, o_ref):
    x = x_ref[...]
    a = _silu(jnp.dot(x, w1_ref[...], preferred_element_type=F32))
    a = a * jnp.dot(x, w3_ref[...], preferred_element_type=F32)
    o_ref[...] = jnp.dot(a.astype(BF16), w2_ref[...], preferred_element_type=F32)


def _shared_ffn(x, w1, w3, w2):
    T, D = x.shape
    F = w1.shape[1]
    tm = _pick(T, (640, 512, 256, 128))
    return pl.pallas_call(
        _ffn_body,
        name="shared_ffn",
        grid=(T // tm,),
        in_specs=[pl.BlockSpec((tm, D), lambda r: (r, 0)),
                  pl.BlockSpec((D, F), lambda r: (0, 0)), pl.BlockSpec((D, F), lambda r: (0, 0)),
                  pl.BlockSpec((F, D), lambda r: (0, 0))],
        out_specs=pl.BlockSpec((tm, D), lambda r: (r, 0)),
        out_shape=jax.ShapeDtypeStruct((T, D), F32),
        compiler_params=_params("arbitrary"),
    )(x, w1, w3, w2)


def _grouped_ffn_body(be_ref, nu_ref, bv_ref, c0_ref, ta_ref, tb_ref, na_ref, nb_ref, hp_hbm,
                      w1_ref, w3_ref, w2_ref, o_ref, xbuf, sem, h1_ref, h3_ref, act_ref):
    del be_ref
    i = pl.program_id(0)
    p = pl.program_id(1)
    n_used = nu_ref[0]
    used = i < n_used
    n_valid = bv_ref[i]
    slot = lax.rem(i, 2)
    subs = [(s0, min(MOE_SUB, MOE_BLK - s0)) for s0 in range(0, MOE_BLK, MOE_SUB)]

    def row_copy(tok, slot_, r):
        return pltpu.make_async_copy(hp_hbm.at[pl.ds(tok, 1), :], xbuf.at[slot_, pl.ds(r, 1), :], sem.at[slot_])

    def request(blk, slot_, ca_ref, cb_ref, r_lo, n_rows):
        off = lax.rem(c0_ref[blk], TOK_CHUNK)

        def body(t, carry):
            r = r_lo + t
            j = off + r
            tok = jnp.where(j < TOK_CHUNK, ca_ref[0, jnp.minimum(j, TOK_CHUNK - 1)],
                            cb_ref[0, jnp.maximum(j - TOK_CHUNK, 0)])
            row_copy(tok, slot_, r).start()
            return carry

        lax.fori_loop(0, n_rows, body, 0, unroll=ROW_UNROLL)

    @pl.when(jnp.logical_and(i == 0, p == 0))
    def _prime():
        request(0, 0, ta_ref, tb_ref, 0, MOE_BLK)

    @pl.when(i + 1 < n_used)
    def _prefetch():
        q = MOE_BLK // 2

        @pl.when(p >= 2)
        def _part():
            request(i + 1, 1 - slot, na_ref, nb_ref, (p - 2) * q, q)

    @pl.when(jnp.logical_and(used, p == 0))
    def _await_rows():
        def body(t, carry):
            row_copy(0, slot, 0).wait()
            return carry

        lax.fori_loop(0, MOE_BLK, body, 0, unroll=ROW_UNROLL)

    @pl.when(jnp.logical_and(used, p < 2))
    def _up():
        w1 = w1_ref[...].astype(BF16)
        w3 = w3_ref[...].astype(BF16)
        for s0, sz in subs:
            @pl.when(s0 < n_valid)
            def _sub(s0=s0, sz=sz):
                xu = xbuf[slot, s0:s0 + sz, :]
                xbits = jnp.where(p == 0, jnp.left_shift(xu, jnp.uint32(16)),
                                  jnp.bitwise_and(xu, jnp.uint32(0xFFFF0000)))
                x = pltpu.bitcast(xbits, F32).astype(BF16)
                a1 = jnp.dot(x, w1, preferred_element_type=F32)
                a3 = jnp.dot(x, w3, preferred_element_type=F32)

                @pl.when(p == 0)
                def _first():
                    h1_ref[s0:s0 + sz, :] = a1
                    h3_ref[s0:s0 + sz, :] = a3

                @pl.when(p == 1)
                def _second():
                    act_ref[s0:s0 + sz, :] = (_silu(h1_ref[s0:s0 + sz, :] + a1)
                                              * (h3_ref[s0:s0 + sz, :] + a3)).astype(BF16)

    @pl.when(jnp.logical_and(used, p >= 2))
    def _down():
        w2 = w2_ref[...].astype(BF16)
        for s0, sz in subs:
            @pl.when(s0 < n_valid)
            def _sub(s0=s0, sz=sz):
                o_ref[s0:s0 + sz, :] = jnp.dot(act_ref[s0:s0 + sz, :], w2,
                                               preferred_element_type=F32).astype(o_ref.dtype)

            @pl.when(s0 >= n_valid)
            def _pad(s0=s0, sz=sz):
                o_ref[s0:s0 + sz, :] = jnp.zeros((sz, o_ref.shape[1]), o_ref.dtype)

    @pl.when(jnp.logical_not(used))
    def _unused_block():
        o_ref[...] = jnp.zeros_like(o_ref)


def _grouped_ffn(h_packed, tok_chunks, c0, block_e, n_used, block_valid, w1, w3, w2):
    T, Dh = h_packed.shape
    D = 2 * Dh
    E, _, F = w1.shape
    nb = block_e.shape[0]
    n_chunks = tok_chunks.shape[0]

    def last_used(i, nu):
        return jnp.minimum(i, nu[0] - 1)

    def k_half(i, p, nu):
        return jnp.where(i < nu[0], jnp.minimum(p, 1), 1)

    def w2_index(i, p, be, nu, bv, c0_):
        live = i < nu[0]
        blk = jnp.where(jnp.logical_and(live, p < 2), jnp.maximum(i - 1, 0), i)
        return (be[blk], 0, jnp.where(jnp.logical_and(live, p == 2), 0, 1))

    def chunk(blk_of, plus):
        def index(i, p, be, nu, bv, c0_):
            q = c0_[last_used(blk_of(i), nu)] // TOK_CHUNK + plus
            return (jnp.minimum(q, n_chunks - 1), 0, 0)
        return pl.BlockSpec((None, 1, TOK_CHUNK), index, memory_space=pltpu.SMEM)

    return pl.pallas_call(
        _grouped_ffn_body,
        name="moe_grouped_ffn",
        grid_spec=pltpu.PrefetchScalarGridSpec(
            num_scalar_prefetch=4,
            grid=(nb, 4),
            in_specs=[
                chunk(lambda i: i, 0), chunk(lambda i: i, 1),
                chunk(lambda i: i + 1, 0), chunk(lambda i: i + 1, 1),
                pl.BlockSpec(memory_space=pl.ANY),
                pl.BlockSpec((None, Dh, F), lambda i, p, be, nu, bv, c0_: (be[i], k_half(i, p, nu), 0)),
                pl.BlockSpec((None, Dh, F), lambda i, p, be, nu, bv, c0_: (be[i], k_half(i, p, nu), 0)),
                pl.BlockSpec((None, F, Dh), w2_index),
            ],
            out_specs=pl.BlockSpec((MOE_BLK, Dh),
                                   lambda i, p, be, nu, bv, c0_: (jnp.minimum(i, nu[0]),
                                                                  jnp.where(i < nu[0], jnp.maximum(p - 2, 0), 0))),
            scratch_shapes=[pltpu.VMEM((2, MOE_BLK, Dh), jnp.uint32), pltpu.SemaphoreType.DMA((2,)),
                            pltpu.VMEM((MOE_BLK, F), F32), pltpu.VMEM((MOE_BLK, F), F32),
                            pltpu.VMEM((MOE_BLK, F), BF16)],
        ),
        out_shape=jax.ShapeDtypeStruct(((nb + 1) * MOE_BLK, D), BF16),
        compiler_params=_params("arbitrary", "arbitrary"),
    )(block_e, n_used, block_valid, c0, tok_chunks, tok_chunks, tok_chunks, tok_chunks, h_packed, w1, w3, w2)


def _moe_routed(h_packed, idx, wts, w1, w3, w2):
    T = h_packed.shape[0]
    E = w1.shape[0]
    A = T * TOP_K
    nb = -(-A // MOE_BLK) + E
    i32 = jnp.int32
    flat_t = jnp.repeat(jnp.arange(T, dtype=i32), TOP_K)
    _, st = lax.sort((idx.reshape(-1), flat_t), num_keys=1, is_stable=True)
    e_iota = jnp.arange(E, dtype=i32)
    onehot = idx[:, :, None] == e_iota[None, None, :]
    csum = jnp.cumsum(onehot.any(axis=1).astype(i32), axis=0)
    counts = csum[-1]
    pcounts = (counts + MOE_BLK - 1) // MOE_BLK * MOE_BLK
    start = jnp.cumsum(counts) - counts
    pend = jnp.cumsum(pcounts)
    pstart = pend - pcounts
    n_used = (pend[-1:] // MOE_BLK).astype(i32)
    blk0 = jnp.arange(nb, dtype=i32) * MOE_BLK
    block_e = jnp.minimum(jnp.searchsorted(pend, blk0, side='right'), E - 1).astype(i32)
    block_e = block_e[jnp.minimum(jnp.arange(nb), n_used[0] - 1)]
    off_in_e = blk0 - pstart[block_e]
    block_valid = jnp.clip(counts[block_e] - off_in_e, 0, MOE_BLK).astype(i32)
    c0 = jnp.clip(start[block_e] + off_in_e, 0, A).astype(i32)
    n_chunks = -(-A // TOK_CHUNK) + 1
    tok_chunks = jnp.pad(st, (0, n_chunks * TOK_CHUNK - A)).reshape(n_chunks, 1, TOK_CHUNK)
    y_sorted = _grouped_ffn(h_packed, tok_chunks, c0, block_e, n_used, block_valid, w1, w3, w2)
    slot = (pstart[None, :] + csum - 1)[:, None, :]
    dest = jnp.sum(jnp.where(onehot, slot, 0), axis=-1)
    y_tok = jnp.take(y_sorted, dest, axis=0, mode="clip")
    return jnp.sum(y_tok.astype(F32) * wts[:, :, None], axis=1)


def _rope_tables(pos):
    half = ROT_DIM // 2
    inv = jnp.power(ROPE_THETA, -jnp.arange(half, dtype=F32) / half)
    ang = pos.astype(F32)[:, None] * inv[None, :]
    cos, sin = jnp.cos(ang), jnp.sin(ang)
    n = pos.shape[0]
    ones = jnp.ones((n, HEAD_DIM - ROT_DIM), F32)
    zeros = jnp.zeros((n, HEAD_DIM - half), F32)
    cos_t = jnp.concatenate([cos, cos, ones], axis=1)
    sin_a = jnp.concatenate([-sin, zeros], axis=1)
    sin_b = jnp.concatenate([jnp.zeros((n, half), F32), sin, jnp.zeros((n, HEAD_DIM - ROT_DIM), F32)], axis=1)
    return cos_t, sin_a, sin_b


def kernel(x_prompt, x_sample, c_prompt, c_sample, cache_k, cache_v, state_ssm, state_conv, page_table, w_ada, b_ada, norm_mix_pre, norm_mix_post, norm_ffn_pre, norm_ffn_post, w_in, conv_w, a_log, dt_bias, gdn_norm, w_out, w_router, e_bias, w1, w3, w2, ws1, ws3, ws2):
    B, S, D = x_prompt.shape
    Bs = x_sample.shape[0]
    assert x_sample.shape[1] == 1 and w_ada.shape[0] == 1
    assert Bs == ROW_BLK and S % MOBA_BLOCK == 0 and S % ROW_BLK == 0
    Tp = B * S
    T = Tp + Bs
    rb_per_seq = S // ROW_BLK
    past_len = page_table.shape[1] * PAGE_SIZE
    H = GDN_HEADS

    x_all = jnp.concatenate([x_prompt.reshape(Tp, D), x_sample.reshape(Bs, D)], axis=0)

    c_all = jnp.concatenate([c_prompt, c_sample], axis=0)
    n_c = c_all.shape[0]
    c_pad = jnp.pad(c_all, ((0, (-n_c) % 8), (0, 0)))
    mod = _matmul(c_pad, w_ada[0], b_ada, name="adaln", silu_in=True, tm=c_pad.shape[0])
    mod_rows = jnp.concatenate([jnp.repeat(mod[:B], ROW_BLK, axis=0), mod[B:B + Bs]], axis=0)

    h = _prenorm(x_all, norm_mix_pre, mod_rows, 0, 1, rb_per_seq)
    w_in0 = w_in[0]
    proj = _matmul(h, w_in0, name="in_proj", n_cols=OFF_AB)
    w_ab = jnp.pad(w_in0[:, OFF_AB:], ((0, 0), (0, HEAD_DIM - 2 * H)))
    ab = _matmul(h, w_ab, name="in_proj_gates")

    pos = jnp.concatenate([jnp.tile(jnp.arange(S, dtype=jnp.int32), B),
                           jnp.full((Bs,), past_len, jnp.int32)])
    qk_rot = _rope(proj, *_rope_tables(pos))
    k_rot = qk_rot[:, Q_ATTN:]
    v_new = proj[:, OFF_V:OFF_V + KV_ATTN]
    o_a_p = _attn_prefill(qk_rot, proj, B, S)
    grp = ATTN_HEADS // KV_HEADS
    q_s = qk_rot[Tp:, :Q_ATTN].reshape(Bs, ATTN_HEADS, HEAD_DIM)
    kx = jnp.repeat(k_rot[Tp:].reshape(Bs, KV_HEADS, HEAD_DIM), grp, axis=1)
    vx = jnp.repeat(v_new[Tp:].reshape(Bs, KV_HEADS, HEAD_DIM), grp, axis=1)
    o_a_s = _attn_decode(q_s, kx, vx, cache_k.reshape(-1, HEAD_DIM), cache_v.reshape(-1, HEAD_DIM), page_table)
    o_a = jnp.concatenate([o_a_p, o_a_s.reshape(Bs, Q_ATTN)], axis=0)

    alog_pad = jnp.pad(a_log.astype(F32), ((0, 0), (0, HEAD_DIM - H)))
    dtb_pad = jnp.pad(dt_bias.astype(F32), ((0, 0), (0, HEAD_DIM - H)))
    gb = _gates(ab, alog_pad, dtb_pad)
    qkv_p = _conv_prompt(proj, conv_w[0], B, S)
    buf_t = state_conv[0].transpose(1, 0, 2)
    qkv_s = _conv_sample(proj, buf_t, conv_w[0], Tp // Bs)
    n_chunks = S // GDN_CHUNK

    def chunk_rows(a):
        return a.reshape(B, n_chunks, GDN_CHUNK, H).transpose(0, 1, 3, 2)

    o_g_p, ssm_p = _gdn_prompt(qkv_p, chunk_rows(gb[:Tp, :H]), chunk_rows(gb[:Tp, H:2 * H]), B, S)
    o_g_s, ssm_s = _gdn_sample(qkv_s.reshape(Bs, 3 * H, HEAD_DIM), gb[Tp:].reshape(Bs, 1, HEAD_DIM),
                               state_ssm[0].astype(F32))
    o_g = jnp.concatenate([o_g_p, o_g_s.reshape(Bs, V_GDN)], axis=0)
    o_gn = _gated_norm(o_g, proj, gdn_norm)

    mix = _matmul(jnp.concatenate([o_a, o_gn], axis=1), w_out[0], name="out_proj")

    x1, h2, h2_packed, logits = _mid(x_all, mix, norm_mix_post, norm_ffn_pre, mod_rows, w_router[0], rb_per_seq)
    idx_l, wts_l = _route(logits, e_bias)
    routed = _moe_routed(h2_packed, idx_l[:, :TOP_K], wts_l[:, :TOP_K], w1[0], w3[0], w2[0])
    shared = _shared_ffn(h2, ws1[0].astype(BF16), ws3[0].astype(BF16), ws2[0].astype(BF16))
    y_all = _final(x1, routed, shared, norm_ffn_post, mod_rows, rb_per_seq)

    y_prompt = y_all[:Tp].reshape(B, S, D)
    y_sample = y_all[Tp:].reshape(Bs, 1, D)
    k_prompt = k_rot[:Tp].reshape(1, B, S, KV_HEADS, HEAD_DIM)
    v_prompt = v_new[:Tp].reshape(1, B, S, KV_HEADS, HEAD_DIM)
    k_sample = k_rot[Tp:].reshape(1, Bs, 1, KV_HEADS, HEAD_DIM)
    v_sample = v_new[Tp:].reshape(1, Bs, 1, KV_HEADS, HEAD_DIM)
    def pre_conv_rows(r0, r1):
        return lax.slice(proj, (r0, OFF_CONV), (r1, OFF_CONV + GDN_CONV_CH))

    conv_prompt = jnp.stack([pre_conv_rows((b + 1) * S - (CONV_W - 1), (b + 1) * S) for b in range(B)])[None]
    conv_sample = jnp.concatenate([state_conv[0][:, 1:], pre_conv_rows(Tp, T)[:, None, :]], axis=1)[None]
    return (y_prompt, y_sample, k_prompt, v_prompt, k_sample, v_sample,
            ssm_p[None].astype(state_ssm.dtype), ssm_s[None].astype(state_ssm.dtype),
            conv_prompt.astype(state_conv.dtype), conv_sample.astype(state_conv.dtype))
```

```python
import functools

import jax
import jax.numpy as jnp
from jax import lax
from jax.experimental import pallas as pl
from jax.experimental.pallas import tpu as pltpu

F32 = jnp.float32
BF16 = jnp.bfloat16
HIGHEST = lax.Precision.HIGHEST

HEAD_DIM = 128
ATTN_HEADS = 16
KV_HEADS = 4
GDN_HEADS = 16
CONV_W = 4
GDN_CHUNK = 64
MOBA_BLOCK = 256
MOBA_TOPK = 3
ROPE_THETA = 500000.0
ROT_DIM = HEAD_DIM // 4
PAGE_SIZE = 128
N_GROUPS = 8
TOPK_GROUPS = 4
TOP_K = 8
ROUTED_SCALE = 2.5
NORM_EPS = 1e-6

Q_ATTN = ATTN_HEADS * HEAD_DIM
KV_ATTN = KV_HEADS * HEAD_DIM
QK_GDN = GDN_HEADS * HEAD_DIM
V_GDN = GDN_HEADS * HEAD_DIM
GDN_CONV_CH = 2 * QK_GDN + V_GDN
OFF_K = Q_ATTN
OFF_V = Q_ATTN + KV_ATTN
OFF_CONV = Q_ATTN + 2 * KV_ATTN
OFF_Z = OFF_CONV + GDN_CONV_CH
OFF_AB = OFF_Z + V_GDN

ROW_BLK = 128
MOE_BLK = 576
MOE_SUB = 288
TOK_CHUNK = 1024
ROW_UNROLL = 8
NEG = -1e30
VMEM_LIMIT = 56 << 20


def _params(*sem):
    return pltpu.CompilerParams(dimension_semantics=sem, vmem_limit_bytes=VMEM_LIMIT)


def _pick(n, prefs):
    for p in prefs:
        if n % p == 0:
            return p
    return n


def _nt(a, b, **kw):
    return lax.dot_general(a, b, (((1,), (1,)), ((), ())), preferred_element_type=F32, **kw)


def _silu(x):
    return x * jax.nn.sigmoid(x)


def _rms(x, g):
    return x * lax.rsqrt(jnp.mean(x * x, axis=-1, keepdims=True) + NORM_EPS) * g


def _beats(a, b, a_first):
    return jnp.where(a > b, 1.0, jnp.where(a == b, jnp.where(a_first, 1.0, 0.0), 0.0))


def _mm_body(x_ref, w_ref, b_ref, o_ref, wbf_ref, *, silu_in):
    @pl.when(pl.program_id(1) == 0)
    def _cast_weight_tile():
        wbf_ref[...] = w_ref[...].astype(BF16)

    x = x_ref[...]
    if silu_in:
        x = _silu(x.astype(F32))
    acc = jnp.dot(x.astype(BF16), wbf_ref[...], preferred_element_type=F32)
    o_ref[...] = (acc + b_ref[...]).astype(o_ref.dtype)


def _matmul(x, w, b=None, *, name, n_cols=None, silu_in=False, out_dtype=F32, tm=None, tn=None):
    M, K = x.shape
    N = n_cols or w.shape[1]
    tm = tm or _pick(M, (640, 512, 256, 128))
    tn = tn or _pick(N, (512, 256, 128))
    if b is None:
        b = jnp.zeros((1, N), F32)
    return pl.pallas_call(
        functools.partial(_mm_body, silu_in=silu_in),
        name=name,
        grid=(N // tn, M // tm),
        in_specs=[
            pl.BlockSpec((tm, K), lambda n, m: (m, 0)),
            pl.BlockSpec((K, tn), lambda n, m: (0, n)),
            pl.BlockSpec((1, tn), lambda n, m: (0, n)),
        ],
        out_specs=pl.BlockSpec((tm, tn), lambda n, m: (m, n)),
        out_shape=jax.ShapeDtypeStruct((M, N), out_dtype),
        scratch_shapes=[pltpu.VMEM((K, tn), BF16)],
        compiler_params=_params("arbitrary", "arbitrary"),
    )(x, w, b)


def _prenorm_body(x_ref, g_ref, sh_ref, sc_ref, o_ref):
    y = _rms(x_ref[...], g_ref[...]) * (1.0 + sc_ref[...]) + sh_ref[...]
    o_ref[...] = y.astype(o_ref.dtype)


def _mod_spec(D, slot, rb_per_seq):
    return pl.BlockSpec((ROW_BLK, D), lambda r: (r // rb_per_seq, slot))


def _row_spec(D):
    return pl.BlockSpec((ROW_BLK, D), lambda r: (r, 0))


def _vec_spec(D):
    return pl.BlockSpec((1, D), lambda r: (0, 0))


def _prenorm(x, g, mod, slot_sh, slot_sc, rb_per_seq):
    T, D = x.shape
    return pl.pallas_call(
        _prenorm_body,
        name="prenorm",
        grid=(T // ROW_BLK,),
        in_specs=[_row_spec(D), _vec_spec(D), _mod_spec(D, slot_sh, rb_per_seq),
                  _mod_spec(D, slot_sc, rb_per_seq)],
        out_specs=_row_spec(D),
        out_shape=jax.ShapeDtypeStruct((T, D), BF16),
        compiler_params=_params("arbitrary"),
    )(x, g, mod, mod)


def _mid_body(x_ref, mix_ref, gpost_ref, gt_ref, gpre_ref, sh_ref, sc_ref, wr_ref,
              x1_ref, h_ref, hp_ref, lg_ref):
    x1 = x_ref[...] + gt_ref[...] * _rms(mix_ref[...], gpost_ref[...])
    x1_ref[...] = x1
    h = _rms(x1, gpre_ref[...]) * (1.0 + sc_ref[...]) + sh_ref[...]
    hb = h.astype(BF16)
    h_ref[...] = hb
    bits = pltpu.bitcast(hb.astype(F32), jnp.uint32)
    dh = bits.shape[1] // 2
    hp_ref[...] = jnp.bitwise_or(jnp.bitwise_and(bits[:, dh:], jnp.uint32(0xFFFF0000)),
                                 jnp.right_shift(bits[:, :dh], jnp.uint32(16)))
    lg_ref[...] = jnp.dot(h, wr_ref[...], precision=HIGHEST, preferred_element_type=F32)


def _mid(x, mix, g_post, g_pre, mod, w_router, rb_per_seq):
    T, D = x.shape
    E = w_router.shape[1]
    return pl.pallas_call(
        _mid_body,
        name="mid_norms_router",
        grid=(T // ROW_BLK,),
        in_specs=[_row_spec(D), _row_spec(D), _vec_spec(D), _mod_spec(D, 2, rb_per_seq),
                  _vec_spec(D), _mod_spec(D, 3, rb_per_seq), _mod_spec(D, 4, rb_per_seq),
                  pl.BlockSpec((D, E), lambda r: (0, 0))],
        out_specs=[_row_spec(D), _row_spec(D), _row_spec(D // 2), _row_spec(E)],
        out_shape=[jax.ShapeDtypeStruct((T, D), F32), jax.ShapeDtypeStruct((T, D), BF16),
                   jax.ShapeDtypeStruct((T, D // 2), jnp.uint32), jax.ShapeDtypeStruct((T, E), F32)],
        compiler_params=_params("arbitrary"),
    )(x, mix, g_post, mod, g_pre, mod, mod, w_router)


def _final_body(x1_ref, a_ref, b_ref, g_ref, gt_ref, o_ref):
    f = a_ref[...] + b_ref[...]
    o_ref[...] = x1_ref[...] + gt_ref[...] * _rms(f, g_ref[...])


def _final(x1, routed, shared, g_post, mod, rb_per_seq):
    T, D = x1.shape
    return pl.pallas_call(
        _final_body,
        name="final_norm",
        grid=(T // ROW_BLK,),
        in_specs=[_row_spec(D), _row_spec(D), _row_spec(D), _vec_spec(D),
                  _mod_spec(D, 5, rb_per_seq)],
        out_specs=_row_spec(D),
        out_shape=jax.ShapeDtypeStruct((T, D), F32),
        compiler_params=_params("arbitrary"),
    )(x1, routed, shared, g_post, mod)


def _rope_body(x_ref, c_ref, sa_ref, sb_ref, o_ref):
    half = ROT_DIM // 2
    c, sa, sb = c_ref[...], sa_ref[...], sb_ref[...]
    for h in range(x_ref.shape[1] // HEAD_DIM):
        cols = slice(h * HEAD_DIM, (h + 1) * HEAD_DIM)
        x = x_ref[:, cols]
        o_ref[:, cols] = x * c + pltpu.roll(x, HEAD_DIM - half, 1) * sa + pltpu.roll(x, half, 1) * sb


def _rope(proj, cos_t, sin_a, sin_b):
    T = proj.shape[0]
    width = (ATTN_HEADS + KV_HEADS) * HEAD_DIM
    tr = _pick(T, (640, 512, 256, 128))
    tab = pl.BlockSpec((tr, HEAD_DIM), lambda r: (r, 0))
    blk = pl.BlockSpec((tr, width), lambda r: (r, 0))
    return pl.pallas_call(
        _rope_body,
        name="rope",
        grid=(T // tr,),
        in_specs=[blk, tab, tab, tab],
        out_specs=blk,
        out_shape=jax.ShapeDtypeStruct((T, width), F32),
        compiler_params=_params("arbitrary"),
    )(proj, cos_t, sin_a, sin_b)


def _topk_mask(gate, lane, n_valid, n_cand):
    gate = jnp.where(lane < n_valid, gate, -jnp.inf)
    rank = jnp.zeros(gate.shape, F32)
    for m in range(n_cand):
        rank = rank + _beats(gate[:, m:m + 1], gate, m < lane)
    return jnp.where(lane < n_valid, jnp.where(rank < MOBA_TOPK, 1.0, 0.0), 0.0)


def _attn_prefill_body(q_ref, k_ref, v_ref, o_ref, s_ref, mx_ref, ls_ref, acc_ref, sel_ref, *, nblk, grp):
    i = pl.program_id(2)
    blk = MOBA_BLOCK
    rows = grp * blk
    scale = HEAD_DIM ** -0.5
    qf = jnp.concatenate([q_ref[:, h * HEAD_DIM:(h + 1) * HEAD_DIM] for h in range(grp)], axis=0)
    qb = qf.astype(BF16)

    krow = lax.broadcasted_iota(jnp.int32, (HEAD_DIM, HEAD_DIM), 0)
    kmean = jnp.zeros((HEAD_DIM, HEAD_DIM), F32)
    for n in range(nblk):
        kmean = jnp.where(krow == n, jnp.mean(k_ref[n * blk:(n + 1) * blk, :], axis=0, keepdims=True), kmean)
    gate = _nt(qf, kmean, precision=HIGHEST)
    lane = lax.broadcasted_iota(jnp.int32, (rows, HEAD_DIM), 1)
    sel_ref[...] = _topk_mask(gate, lane, i, nblk)

    start = pl.multiple_of(i * blk, blk)
    s = _nt(qb, k_ref[pl.ds(start, blk), :].astype(BF16)) * scale
    qpos = jnp.bitwise_and(lax.broadcasted_iota(jnp.int32, (rows, blk), 0), blk - 1)
    kpos = lax.broadcasted_iota(jnp.int32, (rows, blk), 1)
    s = jnp.where(kpos <= qpos, s, NEG)
    s_ref[nblk - 1] = s
    mx_ref[...] = s

    for n in range(nblk - 1):
        @pl.when(n < i)
        def _scores(n=n):
            s = _nt(qb, k_ref[n * blk:(n + 1) * blk, :].astype(BF16)) * scale
            s = jnp.where(sel_ref[:, n:n + 1] > 0.0, s, NEG)
            s_ref[n] = s
            mx_ref[...] = jnp.maximum(mx_ref[...], s)

    m = jnp.max(mx_ref[...], axis=1, keepdims=True)
    p = jnp.exp(s_ref[nblk - 1] - m)
    ls_ref[...] = p
    acc_ref[...] = jnp.dot(p.astype(BF16), v_ref[pl.ds(start, blk), :].astype(BF16),
                           preferred_element_type=F32)

    for n in range(nblk - 1):
        @pl.when(n < i)
        def _weights(n=n):
            p = jnp.exp(s_ref[n] - m)
            ls_ref[...] = ls_ref[...] + p
            acc_ref[...] = acc_ref[...] + jnp.dot(
                p.astype(BF16), v_ref[n * blk:(n + 1) * blk, :].astype(BF16),
                preferred_element_type=F32)

    out = acc_ref[...] / jnp.sum(ls_ref[...], axis=1, keepdims=True)
    for h in range(grp):
        o_ref[:, h * HEAD_DIM:(h + 1) * HEAD_DIM] = out[h * blk:(h + 1) * blk].astype(o_ref.dtype)


def _attn_prefill(qk_rot, proj, B, S):
    nblk = S // MOBA_BLOCK
    grp = ATTN_HEADS // KV_HEADS
    rows = grp * MOBA_BLOCK
    k_col0 = OFF_K // HEAD_DIM
    v_col0 = OFF_V // HEAD_DIM
    return pl.pallas_call(
        functools.partial(_attn_prefill_body, nblk=nblk, grp=grp),
        name="attn_prefill",
        grid=(B, KV_HEADS, nblk),
        in_specs=[
            pl.BlockSpec((MOBA_BLOCK, grp * HEAD_DIM), lambda b, g, i: (b * nblk + i, g)),
            pl.BlockSpec((S, HEAD_DIM), lambda b, g, i: (b, k_col0 + g)),
            pl.BlockSpec((S, HEAD_DIM), lambda b, g, i: (b, v_col0 + g)),
        ],
        out_specs=pl.BlockSpec((MOBA_BLOCK, grp * HEAD_DIM), lambda b, g, i: (b * nblk + i, g)),
        out_shape=jax.ShapeDtypeStruct((B * S, Q_ATTN), BF16),
        scratch_shapes=[pltpu.VMEM((nblk, rows, MOBA_BLOCK), F32), pltpu.VMEM((rows, MOBA_BLOCK), F32),
                        pltpu.VMEM((rows, MOBA_BLOCK), F32), pltpu.VMEM((rows, HEAD_DIM), F32),
                        pltpu.VMEM((rows, HEAD_DIM), F32)],
        compiler_params=_params("arbitrary", "arbitrary", "arbitrary"),
    )(qk_rot, qk_rot, proj)


def _attn_decode_body(pt_ref, q_ref, kx_ref, vx_ref, *refs, n_pages):
    del pt_ref
    k_pages, v_pages, o_ref = refs[:n_pages], refs[n_pages:2 * n_pages], refs[2 * n_pages]
    H = ATTN_HEADS
    grp = ATTN_HEADS // KV_HEADS
    ppb = MOBA_BLOCK // PAGE_SIZE
    nblk = n_pages // ppb
    page_rows = PAGE_SIZE * KV_HEADS
    blk_cols = MOBA_BLOCK * KV_HEADS
    ncol = n_pages * page_rows
    scale = HEAD_DIM ** -0.5
    q = q_ref[...]
    row1 = lax.broadcasted_iota(jnp.int32, (H, 1), 0)
    lane = lax.broadcasted_iota(jnp.int32, (H, HEAD_DIM), 1)

    gate = jnp.zeros((H, HEAD_DIM), F32)
    for n in range(nblk):
        ks8 = k_pages[n * ppb][...].reshape(page_rows // 8, 8, HEAD_DIM).sum(axis=0)
        for j in range(1, ppb):
            ks8 = ks8 + k_pages[n * ppb + j][...].reshape(page_rows // 8, 8, HEAD_DIM).sum(axis=0)
        ksum = ks8[:KV_HEADS] + ks8[KV_HEADS:]
        kexp = jnp.zeros((H, HEAD_DIM), F32)
        for h in range(KV_HEADS):
            mine_h = jnp.logical_and(row1 >= h * grp, row1 < (h + 1) * grp)
            kexp = jnp.where(mine_h, ksum[h:h + 1, :], kexp)
        g_n = jnp.sum(q * kexp, axis=1, keepdims=True) / MOBA_BLOCK
        gate = jnp.where(lane == n, g_n, gate)
    sel = _topk_mask(gate, lane, nblk, nblk)

    kall = jnp.concatenate([kp[...].astype(BF16) for kp in k_pages], axis=0)
    s = _nt(q.astype(BF16), kall) * scale
    col = lax.broadcasted_iota(jnp.int32, (H, ncol), 1)
    row = lax.broadcasted_iota(jnp.int32, (H, ncol), 0)
    own_head = jnp.bitwise_and(col, KV_HEADS - 1) * grp == jnp.bitwise_and(row, -grp)
    picked = jnp.concatenate(
        [jnp.broadcast_to(sel[:, n:n + 1], (H, blk_cols)) for n in range(nblk)], axis=1)
    s = jnp.where(jnp.logical_and(own_head, picked > 0.0), s, NEG)
    s_own = jnp.sum(q * kx_ref[...], axis=1, keepdims=True) * scale
    m = jnp.maximum(jnp.max(s, axis=1, keepdims=True), s_own)
    p = jnp.exp(s - m)
    e_own = jnp.exp(s_own - m)
    denom = jnp.sum(p, axis=1, keepdims=True) + e_own
    vall = jnp.concatenate([vp[...].astype(BF16) for vp in v_pages], axis=0)
    out = (jnp.dot(p.astype(BF16), vall, preferred_element_type=F32) + e_own * vx_ref[...]) / denom
    o_ref[...] = out.astype(o_ref.dtype)


def _attn_decode(q_s, kx, vx, cache_k, cache_v, page_table):
    Bs, n_pages = page_table.shape
    assert n_pages % (MOBA_BLOCK // PAGE_SIZE) == 0
    assert KV_HEADS & (KV_HEADS - 1) == 0 and (ATTN_HEADS // KV_HEADS) & (ATTN_HEADS // KV_HEADS - 1) == 0
    vec = pl.BlockSpec((None, ATTN_HEADS, HEAD_DIM), lambda b, pt: (b, 0, 0))

    def page(j):
        return pl.BlockSpec((PAGE_SIZE * KV_HEADS, HEAD_DIM), lambda b, pt: (pt[b, j], 0))

    pages = [page(j) for j in range(n_pages)]
    return pl.pallas_call(
        functools.partial(_attn_decode_body, n_pages=n_pages),
        name="attn_decode",
        grid_spec=pltpu.PrefetchScalarGridSpec(
            num_scalar_prefetch=1,
            grid=(Bs,),
            in_specs=[vec, vec, vec] + pages + pages,
            out_specs=vec,
        ),
        out_shape=jax.ShapeDtypeStruct((Bs, ATTN_HEADS, HEAD_DIM), BF16),
        compiler_params=_params("arbitrary"),
    )(page_table, q_s, kx, vx, *([cache_k] * n_pages), *([cache_v] * n_pages))


def _conv_finish(y, c):
    y = _silu(y)
    nrm = y * lax.rsqrt(jnp.sum(y * y, axis=1, keepdims=True) + 1e-6)
    return jnp.where(c < GDN_HEADS, nrm * (HEAD_DIM ** -0.5), jnp.where(c < 2 * GDN_HEADS, nrm, y))


CONV_TILE = 4 * HEAD_DIM


def _conv_prompt_body(x_ref, w_ref, o_ref):
    n = x_ref.shape[1] // HEAD_DIM
    row = lax.broadcasted_iota(jnp.int32, (x_ref.shape[0], HEAD_DIM), 0)
    for j in range(n):
        cols = slice(j * HEAD_DIM, (j + 1) * HEAD_DIM)
        x = x_ref[:, cols]
        y = x * w_ref[CONV_W - 1:CONV_W, cols]
        for k in range(1, CONV_W):
            y = y + jnp.where(row >= k, pltpu.roll(x, k, 0), 0.0) * w_ref[CONV_W - 1 - k:CONV_W - k, cols]
        o_ref[:, cols] = _conv_finish(y, pl.program_id(1) * n + j)


def _conv_prompt(proj, conv_w, B, S):
    assert OFF_CONV % CONV_TILE == 0 and GDN_CONV_CH % CONV_TILE == 0
    n_tiles = GDN_CONV_CH // CONV_TILE
    col0 = OFF_CONV // CONV_TILE
    return pl.pallas_call(
        _conv_prompt_body,
        name="conv_prompt",
        grid=(B, n_tiles),
        in_specs=[pl.BlockSpec((S, CONV_TILE), lambda b, c: (b, col0 + c)),
                  pl.BlockSpec((CONV_W, CONV_TILE), lambda b, c: (0, c))],
        out_specs=pl.BlockSpec((S, CONV_TILE), lambda b, c: (b, c)),
        out_shape=jax.ShapeDtypeStruct((B * S, GDN_CONV_CH), F32),
        compiler_params=_params("arbitrary", "arbitrary"),
    )(proj, conv_w)


def _conv_sample_body(x_ref, buf_ref, w_ref, o_ref):
    y = x_ref[...] * w_ref[CONV_W - 1:CONV_W, :]
    for i in range(CONV_W - 1):
        y = y + buf_ref[i] * w_ref[i:i + 1, :]
    o_ref[...] = _conv_finish(y, pl.program_id(0))


def _conv_sample(proj, buf_t, conv_w, row_blk0):
    Bs = buf_t.shape[1]
    n_tiles = GDN_CONV_CH // HEAD_DIM
    col0 = OFF_CONV // HEAD_DIM
    return pl.pallas_call(
        _conv_sample_body,
        name="conv_sample",
        grid=(n_tiles,),
        in_specs=[pl.BlockSpec((Bs, HEAD_DIM), lambda c: (row_blk0, col0 + c)),
                  pl.BlockSpec((CONV_W - 1, Bs, HEAD_DIM), lambda c: (0, 0, c)),
                  pl.BlockSpec((CONV_W, HEAD_DIM), lambda c: (0, c))],
        out_specs=pl.BlockSpec((Bs, HEAD_DIM), lambda c: (0, c)),
        out_shape=jax.ShapeDtypeStruct((Bs, GDN_CONV_CH), F32),
        compiler_params=_params("arbitrary"),
    )(proj, buf_t, conv_w)


def _gates_body(ab_ref, alog_ref, dtb_ref, o_ref):
    x = ab_ref[...]
    sp_in = x + dtb_ref[...]
    softplus = jnp.maximum(sp_in, 0.0) + jnp.log1p(jnp.exp(-jnp.abs(sp_in)))
    g = -jnp.exp(alog_ref[...]) * softplus
    lane = lax.broadcasted_iota(jnp.int32, x.shape, 1)
    o_ref[...] = jnp.where(lane < GDN_HEADS, g, jax.nn.sigmoid(x))


def _gates(ab, alog_pad, dtb_pad):
    T = ab.shape[0]
    return pl.pallas_call(
        _gates_body,
        name="gdn_gates",
        grid=(T // ROW_BLK,),
        in_specs=[_row_spec(HEAD_DIM), _vec_spec(HEAD_DIM), _vec_spec(HEAD_DIM)],
        out_specs=_row_spec(HEAD_DIM),
        out_shape=jax.ShapeDtypeStruct((T, HEAD_DIM), F32),
        compiler_params=_params("arbitrary"),
    )(ab, alog_pad, dtb_pad)


def _mmb(a, b):
    return jnp.dot(a.astype(BF16), b.astype(BF16), preferred_element_type=F32)


def _split_bf16(a):
    hi = a.astype(BF16)
    return hi, (a - hi.astype(F32)).astype(BF16)


def _mm3(a, b):
    ah, al = _split_bf16(a)
    bh, bl = _split_bf16(b)

    def d(x, y):
        return jnp.dot(x, y, preferred_element_type=F32)

    return d(ah, bh) + (d(ah, bl) + d(al, bh))


def _unit_lower_inverse(lows):
    C = lows[0].shape[0]
    ii = lax.broadcasted_iota(jnp.int32, (C, C), 0)
    jj = lax.broadcasted_iota(jnp.int32, (C, C), 1)
    eye = jnp.where(ii == jj, 1.0, 0.0)
    invs = [eye - low for low in lows]
    pws = [_mm3(low, low) for low in lows]
    span = 2
    while span < C:
        invs = [inv + _mm3(inv, pw) for inv, pw in zip(invs, pws)]
        span *= 2
        if span < C:
            pws = [_mm3(pw, pw) for pw in pws]
    return invs


GDN_HEAD_GROUP = 8


def _gdn_chunk_body(q_ref, k_ref, v_ref, g_ref, b_ref, o_ref, s_out_ref, s_ref):
    n = pl.program_id(1)
    C = GDN_CHUNK

    @pl.when(n == 0)
    def _zero_state():
        s_ref[...] = jnp.zeros_like(s_ref)

    ii = lax.broadcasted_iota(jnp.int32, (C, C), 0)
    jj = lax.broadcasted_iota(jnp.int32, (C, C), 1)
    eye = ii == jj
    tril = jj <= ii
    for h0 in range(0, GDN_HEADS, GDN_HEAD_GROUP):
        hs = range(h0, h0 + GDN_HEAD_GROUP)
        cols = [slice(h * HEAD_DIM, (h + 1) * HEAD_DIM) for h in hs]
        q = [q_ref[:, c] for c in cols]
        k = [k_ref[:, c] for c in cols]
        v = [v_ref[:, c] for c in cols]
        gc_col = [jnp.sum(jnp.where(tril, g_ref[h:h + 1, :], 0.0), axis=1, keepdims=True) for h in hs]
        gc_row = [jnp.sum(jnp.where(eye, x, 0.0), axis=0, keepdims=True) for x in gc_col]
        beta = [jnp.sum(jnp.where(eye, b_ref[h:h + 1, :], 0.0), axis=1, keepdims=True) for h in hs]
        decay = [jnp.exp(jnp.where(tril, c_ - r_, -jnp.inf)) for c_, r_ in zip(gc_col, gc_row)]
        g_last = [x[C - 1:C, :] for x in gc_col]
        e_gc = [jnp.exp(x) for x in gc_col]
        kbeta = [k_ * b_ for k_, b_ in zip(k, beta)]
        k16 = [x.astype(BF16) for x in k]
        low = [jnp.where(ii > jj, _nt(kb.astype(BF16), kk) * d, 0.0) for kb, kk, d in zip(kbeta, k16, decay)]
        tmat = [t.astype(BF16) for t in _unit_lower_inverse(low)]
        u = [_mmb(t, v_ * b_) for t, v_, b_ in zip(tmat, v, beta)]
        w = [_mmb(t, kb * e) for t, kb, e in zip(tmat, kbeta, e_gc)]
        attn = [_nt(q_.astype(BF16), kk) * d for q_, kk, d in zip(q, k16, decay)]
        state = [s_ref[h] for h in hs]
        v_new = [u_ - _mmb(w_, s_) for u_, w_, s_ in zip(u, w, state)]
        for c, q_, e, s_, a_, vn in zip(cols, q, e_gc, state, attn, v_new):
            o_ref[:, c] = _mmb(q_ * e, s_) + _mmb(a_, vn)
        for h, k_, gl, gc, s_, vn in zip(hs, k, g_last, gc_col, state, v_new):
            k_tail = k_ * jnp.exp(gl - gc)
            s_ref[h] = s_ * jnp.exp(gl) + _mmb(k_tail.T, vn)

    @pl.when(n == pl.num_programs(1) - 1)
    def _emit_state():
        s_out_ref[...] = s_ref[...]


def _gdn_prompt(qkv, g_rows, b_rows, B, S):
    N = S // GDN_CHUNK
    W = GDN_HEADS * HEAD_DIM

    def part(j):
        return pl.BlockSpec((GDN_CHUNK, W), lambda b, n: (b * N + n, j))

    gspec = pl.BlockSpec((None, None, GDN_HEADS, GDN_CHUNK), lambda b, n: (b, n, 0, 0))
    return pl.pallas_call(
        _gdn_chunk_body,
        name="gdn_prompt",
        grid=(B, N),
        in_specs=[part(0), part(1), part(2), gspec, gspec],
        out_specs=[pl.BlockSpec((GDN_CHUNK, W), lambda b, n: (b * N + n, 0)),
                   pl.BlockSpec((None, GDN_HEADS, HEAD_DIM, HEAD_DIM), lambda b, n: (b, 0, 0, 0))],
        out_shape=[jax.ShapeDtypeStruct((B * S, W), F32),
                   jax.ShapeDtypeStruct((B, GDN_HEADS, HEAD_DIM, HEAD_DIM), F32)],
        scratch_shapes=[pltpu.VMEM((GDN_HEADS, HEAD_DIM, HEAD_DIM), F32)],
        compiler_params=_params("arbitrary", "arbitrary"),
    )(qkv, qkv, qkv, g_rows, b_rows)


def _gdn_step_body(qkv_ref, gb_ref, s_ref, o_ref, s_out_ref):
    H = GDN_HEADS
    ii = lax.broadcasted_iota(jnp.int32, (HEAD_DIM, HEAD_DIM), 0)
    jj = lax.broadcasted_iota(jnp.int32, (HEAD_DIM, HEAD_DIM), 1)
    eye = ii == jj
    lane = lax.broadcasted_iota(jnp.int32, (1, HEAD_DIM), 1)
    gb = gb_ref[...]
    for h in range(H):
        q_col = jnp.sum(jnp.where(eye, qkv_ref[h:h + 1, :], 0.0), axis=1, keepdims=True)
        k_col = jnp.sum(jnp.where(eye, qkv_ref[H + h:H + h + 1, :], 0.0), axis=1, keepdims=True)
        v_row = qkv_ref[2 * H + h:2 * H + h + 1, :]
        g = jnp.sum(jnp.where(lane == h, gb, 0.0), axis=1, keepdims=True)
        beta = jnp.sum(jnp.where(lane == H + h, gb, 0.0), axis=1, keepdims=True)
        state = s_ref[h] * jnp.exp(g)
        kv = jnp.sum(k_col * state, axis=0, keepdims=True)
        delta = (v_row - kv) * beta
        state = state + k_col * delta
        s_out_ref[h] = state
        o_ref[h:h + 1, :] = jnp.sum(q_col * state, axis=0, keepdims=True)


def _gdn_sample(qkv3, gb3, state):
    Bs = qkv3.shape[0]
    H = GDN_HEADS
    sspec = pl.BlockSpec((None, H, HEAD_DIM, HEAD_DIM), lambda b: (b, 0, 0, 0))
    return pl.pallas_call(
        _gdn_step_body,
        name="gdn_sample",
        grid=(Bs,),
        in_specs=[pl.BlockSpec((None, 3 * H, HEAD_DIM), lambda b: (b, 0, 0)),
                  pl.BlockSpec((None, 1, HEAD_DIM), lambda b: (b, 0, 0)), sspec],
        out_specs=[pl.BlockSpec((None, H, HEAD_DIM), lambda b: (b, 0, 0)), sspec],
        out_shape=[jax.ShapeDtypeStruct((Bs, H, HEAD_DIM), F32),
                   jax.ShapeDtypeStruct(state.shape, F32)],
        compiler_params=_params("arbitrary"),
    )(qkv3, gb3, state)


def _gated_norm_body(o_ref, z_ref, g_ref, out_ref):
    for h in range(o_ref.shape[1] // HEAD_DIM):
        c = slice(h * HEAD_DIM, (h + 1) * HEAD_DIM)
        out_ref[:, c] = (_rms(o_ref[:, c], g_ref[...]) * _silu(z_ref[:, c])).astype(out_ref.dtype)


def _gated_norm(o_g, proj, gdn_norm):
    T = o_g.shape[0]
    W = 8 * HEAD_DIM
    assert OFF_Z % W == 0 and V_GDN % W == 0
    tm = _pick(T, (640, 512, 256, 128))
    z_col0 = OFF_Z // W
    return pl.pallas_call(
        _gated_norm_body,
        name="gdn_gated_norm",
        grid=(T // tm, V_GDN // W),
        in_specs=[pl.BlockSpec((tm, W), lambda r, c: (r, c)),
                  pl.BlockSpec((tm, W), lambda r, c: (r, z_col0 + c)),
                  pl.BlockSpec((1, HEAD_DIM), lambda r, c: (0, 0))],
        out_specs=pl.BlockSpec((tm, W), lambda r, c: (r, c)),
        out_shape=jax.ShapeDtypeStruct((T, V_GDN), BF16),
        compiler_params=_params("arbitrary", "arbitrary"),
    )(o_g, proj, gdn_norm)


def _route_body(lg_ref, bias_ref, idx_ref, wt_ref):
    scores = jax.nn.sigmoid(lg_ref[...])
    biased = scores + bias_ref[...]
    shape = scores.shape
    E = shape[1]
    per_grp = E // N_GROUPS
    lane_i = lax.broadcasted_iota(jnp.int32, shape, 1)
    lane = lane_i.astype(F32)
    ninf = -jnp.inf

    def in_grp(g):
        return jnp.logical_and(lane_i >= g * per_grp, lane_i < (g + 1) * per_grp)

    gscore = []
    for g in range(N_GROUPS):
        xg = jnp.where(in_grp(g), biased, ninf)
        m1 = jnp.max(xg, axis=1, keepdims=True)
        i1 = jnp.min(jnp.where(xg == m1, lane, float(E)), axis=1, keepdims=True)
        m2 = jnp.max(jnp.where(lane == i1, ninf, xg), axis=1, keepdims=True)
        gscore.append(m1 + m2)
    keep = jnp.zeros(shape, F32)
    for g in range(N_GROUPS):
        rank = jnp.zeros_like(gscore[g])
        for o in range(N_GROUPS):
            if o != g:
                rank = rank + _beats(gscore[o], gscore[g], o < g)
        keep = jnp.where(in_grp(g), jnp.where(rank < TOPK_GROUPS, 1.0, 0.0), keep)

    cur = jnp.where(keep > 0.0, biased, ninf)
    idx = jnp.zeros(shape, F32)
    wts = jnp.zeros(shape, F32)
    wsum = jnp.zeros((shape[0], 1), F32)
    for k in range(TOP_K):
        mk = jnp.max(cur, axis=1, keepdims=True)
        ik = jnp.min(jnp.where(cur == mk, lane, float(E)), axis=1, keepdims=True)
        hit = lane == ik
        wk = jnp.sum(jnp.where(hit, scores, 0.0), axis=1, keepdims=True)
        cur = jnp.where(hit, ninf, cur)
        idx = jnp.where(lane_i == k, ik, idx)
        wts = jnp.where(lane_i == k, wk, wts)
        wsum = wsum + wk
    idx_ref[...] = idx.astype(jnp.int32)
    wt_ref[...] = wts / wsum * ROUTED_SCALE


def _route(logits, e_bias):
    T, E = logits.shape
    return pl.pallas_call(
        _route_body,
        name="route",
        grid=(T // ROW_BLK,),
        in_specs=[_row_spec(E), _vec_spec(E)],
        out_specs=[_row_spec(E), _row_spec(E)],
        out_shape=[jax.ShapeDtypeStruct((T, E), jnp.int32), jax.ShapeDtypeStruct((T, E), F32)],
        compiler_params=_params("arbitrary"),
    )(logits, e_bias)


def _ffn_body(x_ref, w1_ref, w3_ref, w2_ref, o_ref):
    x = x_ref[...]
    a = _silu(jnp.dot(x, w1_ref[...], preferred_element_type=F32))
    a = a * jnp.dot(x, w3_ref[...], preferred_element_type=F32)
    o_ref[...] = jnp.dot(a.astype(BF16), w2_ref[...], preferred_element_type=F32)


def _shared_ffn(x, w1, w3, w2):
    T, D = x.shape
    F = w1.shape[1]
    tm = _pick(T, (640, 512, 256, 128))
    return pl.pallas_call(
        _ffn_body,
        name="shared_ffn",
        grid=(T // tm,),
        in_specs=[pl.BlockSpec((tm, D), lambda r: (r, 0)),
                  pl.BlockSpec((D, F), lambda r: (0, 0)), pl.BlockSpec((D, F), lambda r: (0, 0)),
                  pl.BlockSpec((F, D), lambda r: (0, 0))],
        out_specs=pl.BlockSpec((tm, D), lambda r: (r, 0)),
        out_shape=jax.ShapeDtypeStruct((T, D), F32),
        compiler_params=_params("arbitrary"),
    )(x, w1, w3, w2)


def _grouped_ffn_body(be_ref, nu_ref, bv_ref, c0_ref, ta_ref, tb_ref, na_ref, nb_ref, hp_hbm,
                      w1_ref, w3_ref, w2_ref, o_ref, xbuf, sem, h1_ref, h3_ref, act_ref):
    del be_ref
    i = pl.program_id(0)
    p = pl.program_id(1)
    n_used = nu_ref[0]
    used = i < n_used
    n_valid = bv_ref[i]
    slot = lax.rem(i, 2)
    subs = [(s0, min(MOE_SUB, MOE_BLK - s0)) for s0 in range(0, MOE_BLK, MOE_SUB)]

    def row_copy(tok, slot_, r):
        return pltpu.make_async_copy(hp_hbm.at[pl.ds(tok, 1), :], xbuf.at[slot_, pl.ds(r, 1), :], sem.at[slot_])

    def request(blk, slot_, ca_ref, cb_ref, r_lo, n_rows):
        off = lax.rem(c0_ref[blk], TOK_CHUNK)

        def body(t, carry):
            r = r_lo + t
            j = off + r
            tok = jnp.where(j < TOK_CHUNK, ca_ref[0, jnp.minimum(j, TOK_CHUNK - 1)],
                            cb_ref[0, jnp.maximum(j - TOK_CHUNK, 0)])
            row_copy(tok, slot_, r).start()
            return carry

        lax.fori_loop(0, n_rows, body, 0, unroll=ROW_UNROLL)

    @pl.when(jnp.logical_and(i == 0, p == 0))
    def _prime():
        request(0, 0, ta_ref, tb_ref, 0, MOE_BLK)

    @pl.when(i + 1 < n_used)
    def _prefetch():
        q = MOE_BLK // 2

        @pl.when(p >= 2)
        def _part():
            request(i + 1, 1 - slot, na_ref, nb_ref, (p - 2) * q, q)

    @pl.when(jnp.logical_and(used, p == 0))
    def _await_rows():
        def body(t, carry):
            row_copy(0, slot, 0).wait()
            return carry

        lax.fori_loop(0, MOE_BLK, body, 0, unroll=ROW_UNROLL)

    @pl.when(jnp.logical_and(used, p < 2))
    def _up():
        w1 = w1_ref[...].astype(BF16)
        w3 = w3_ref[...].astype(BF16)
        for s0, sz in subs:
            @pl.when(s0 < n_valid)
            def _sub(s0=s0, sz=sz):
                xu = xbuf[slot, s0:s0 + sz, :]
                xbits = jnp.where(p == 0, jnp.left_shift(xu, jnp.uint32(16)),
                                  jnp.bitwise_and(xu, jnp.uint32(0xFFFF0000)))
                x = pltpu.bitcast(xbits, F32).astype(BF16)
                a1 = jnp.dot(x, w1, preferred_element_type=F32)
                a3 = jnp.dot(x, w3, preferred_element_type=F32)

                @pl.when(p == 0)
                def _first():
                    h1_ref[s0:s0 + sz, :] = a1
                    h3_ref[s0:s0 + sz, :] = a3

                @pl.when(p == 1)
                def _second():
                    act_ref[s0:s0 + sz, :] = (_silu(h1_ref[s0:s0 + sz, :] + a1)
                                              * (h3_ref[s0:s0 + sz, :] + a3)).astype(BF16)

    @pl.when(jnp.logical_and(used, p >= 2))
    def _down():
        w2 = w2_ref[...].astype(BF16)
        for s0, sz in subs:
            @pl.when(s0 < n_valid)
            def _sub(s0=s0, sz=sz):
                o_ref[s0:s0 + sz, :] = jnp.dot(act_ref[s0:s0 + sz, :], w2,
                                               preferred_element_type=F32).astype(o_ref.dtype)

            @pl.when(s0 >= n_valid)
            def _pad(s0=s0, sz=sz):
                o_ref[s0:s0 + sz, :] = jnp.zeros((sz, o_ref.shape[1]), o_ref.dtype)

    @pl.when(jnp.logical_not(used))
    def _unused_block():
        o_ref[...] = jnp.zeros_like(o_ref)


def _grouped_ffn(h_packed, tok_chunks, c0, block_e, n_used, block_valid, w1, w3, w2):
    T, Dh = h_packed.shape
    D = 2 * Dh
    E, _, F = w1.shape
    nb = block_e.shape[0]
    n_chunks = tok_chunks.shape[0]

    def last_used(i, nu):
        return jnp.minimum(i, nu[0] - 1)

    def k_half(i, p, nu):
        return jnp.where(i < nu[0], jnp.minimum(p, 1), 1)

    def w2_index(i, p, be, nu, bv, c0_):
        live = i < nu[0]
        blk = jnp.where(jnp.logical_and(live, p < 2), jnp.maximum(i - 1, 0), i)
        return (be[blk], 0, jnp.where(jnp.logical_and(live, p == 2), 0, 1))

    def chunk(blk_of, plus):
        def index(i, p, be, nu, bv, c0_):
            q = c0_[last_used(blk_of(i), nu)] // TOK_CHUNK + plus
            return (jnp.minimum(q, n_chunks - 1), 0, 0)
        return pl.BlockSpec((None, 1, TOK_CHUNK), index, memory_space=pltpu.SMEM)

    return pl.pallas_call(
        _grouped_ffn_body,
        name="moe_grouped_ffn",
        grid_spec=pltpu.PrefetchScalarGridSpec(
            num_scalar_prefetch=4,
            grid=(nb, 4),
            in_specs=[
                chunk(lambda i: i, 0), chunk(lambda i: i, 1),
                chunk(lambda i: i + 1, 0), chunk(lambda i: i + 1, 1),
                pl.BlockSpec(memory_space=pl.ANY),
                pl.BlockSpec((None, Dh, F), lambda i, p, be, nu, bv, c0_: (be[i], k_half(i, p, nu), 0)),
                pl.BlockSpec((None, Dh, F), lambda i, p, be, nu, bv, c0_: (be[i], k_half(i, p, nu), 0)),
                pl.BlockSpec((None, F, Dh), w2_index),
            ],
            out_specs=pl.BlockSpec((MOE_BLK, Dh),
                                   lambda i, p, be, nu, bv, c0_: (jnp.minimum(i, nu[0]),
                                                                  jnp.where(i < nu[0], jnp.maximum(p - 2, 0), 0))),
            scratch_shapes=[pltpu.VMEM((2, MOE_BLK, Dh), jnp.uint32), pltpu.SemaphoreType.DMA((2,)),
                            pltpu.VMEM((MOE_BLK, F), F32), pltpu.VMEM((MOE_BLK, F), F32),
                            pltpu.VMEM((MOE_BLK, F), BF16)],
        ),
        out_shape=jax.ShapeDtypeStruct(((nb + 1) * MOE_BLK, D), BF16),
        compiler_params=_params("arbitrary", "arbitrary"),
    )(block_e, n_used, block_valid, c0, tok_chunks, tok_chunks, tok_chunks, tok_chunks, h_packed, w1, w3, w2)


def _moe_routed(h_packed, idx, wts, w1, w3, w2):
    T = h_packed.shape[0]
    E = w1.shape[0]
    A = T * TOP_K
    nb = -(-A // MOE_BLK) + E
    i32 = jnp.int32
    flat_t = jnp.repeat(jnp.arange(T, dtype=i32), TOP_K)
    _, st = lax.sort((idx.reshape(-1), flat_t), num_keys=1, is_stable=True)
    e_iota = jnp.arange(E, dtype=i32)
    onehot = idx[:, :, None] == e_iota[None, None, :]
    csum = jnp.cumsum(onehot.any(axis=1).astype(i32), axis=0)
    counts = csum[-1]
    pcounts = (counts + MOE_BLK - 1) // MOE_BLK * MOE_BLK
    start = jnp.cumsum(counts) - counts
    pend = jnp.cumsum(pcounts)
    pstart = pend - pcounts
    n_used = (pend[-1:] // MOE_BLK).astype(i32)
    blk0 = jnp.arange(nb, dtype=i32) * MOE_BLK
    block_e = jnp.minimum(jnp.searchsorted(pend, blk0, side='right'), E - 1).astype(i32)
    block_e = block_e[jnp.minimum(jnp.arange(nb), n_used[0] - 1)]
    off_in_e = blk0 - pstart[block_e]
    block_valid = jnp.clip(counts[block_e] - off_in_e, 0, MOE_BLK).astype(i32)
    c0 = jnp.clip(start[block_e] + off_in_e, 0, A).astype(i32)
    n_chunks = -(-A // TOK_CHUNK) + 1
    tok_chunks = jnp.pad(st, (0, n_chunks * TOK_CHUNK - A)).reshape(n_chunks, 1, TOK_CHUNK)
    y_sorted = _grouped_ffn(h_packed, tok_chunks, c0, block_e, n_used, block_valid, w1, w3, w2)
    slot = (pstart[None, :] + csum - 1)[:, None, :]
    dest = jnp.sum(jnp.where(onehot, slot, 0), axis=-1)
    y_tok = jnp.take(y_sorted, dest, axis=0, mode="clip")
    return jnp.sum(y_tok.astype(F32) * wts[:, :, None], axis=1)


def _rope_tables(pos):
    half = ROT_DIM // 2
    inv = jnp.power(ROPE_THETA, -jnp.arange(half, dtype=F32) / half)
    ang = pos.astype(F32)[:, None] * inv[None, :]
    cos, sin = jnp.cos(ang), jnp.sin(ang)
    n = pos.shape[0]
    ones = jnp.ones((n, HEAD_DIM - ROT_DIM), F32)
    zeros = jnp.zeros((n, HEAD_DIM - half), F32)
    cos_t = jnp.concatenate([cos, cos, ones], axis=1)
    sin_a = jnp.concatenate([-sin, zeros], axis=1)
    sin_b = jnp.concatenate([jnp.zeros((n, half), F32), sin, jnp.zeros((n, HEAD_DIM - ROT_DIM), F32)], axis=1)
    return cos_t, sin_a, sin_b


def kernel(x_prompt, x_sample, c_prompt, c_sample, cache_k, cache_v, state_ssm, state_conv, page_table, w_ada, b_ada, norm_mix_pre, norm_mix_post, norm_ffn_pre, norm_ffn_post, w_in, conv_w, a_log, dt_bias, gdn_norm, w_out, w_router, e_bias, w1, w3, w2, ws1, ws3, ws2):
    B, S, D = x_prompt.shape
    Bs = x_sample.shape[0]
    assert x_sample.shape[1] == 1 and w_ada.shape[0] == 1
    assert Bs == ROW_BLK and S % MOBA_BLOCK == 0 and S % ROW_BLK == 0
    Tp = B * S
    T = Tp + Bs
    rb_per_seq = S // ROW_BLK
    past_len = page_table.shape[1] * PAGE_SIZE
    H = GDN_HEADS

    x_all = jnp.concatenate([x_prompt.reshape(Tp, D), x_sample.reshape(Bs, D)], axis=0)

    c_all = jnp.concatenate([c_prompt, c_sample], axis=0)
    n_c = c_all.shape[0]
    c_pad = jnp.pad(c_all, ((0, (-n_c) % 8), (0, 0)))
    mod = _matmul(c_pad, w_ada[0], b_ada, name="adaln", silu_in=True, tm=c_pad.shape[0])
    mod_rows = jnp.concatenate([jnp.repeat(mod[:B], ROW_BLK, axis=0), mod[B:B + Bs]], axis=0)

    h = _prenorm(x_all, norm_mix_pre, mod_rows, 0, 1, rb_per_seq)
    w_in0 = w_in[0]
    proj = _matmul(h, w_in0, name="in_proj", n_cols=OFF_AB)
    w_ab = jnp.pad(w_in0[:, OFF_AB:], ((0, 0), (0, HEAD_DIM - 2 * H)))
    ab = _matmul(h, w_ab, name="in_proj_gates")

    pos = jnp.concatenate([jnp.tile(jnp.arange(S, dtype=jnp.int32), B),
                           jnp.full((Bs,), past_len, jnp.int32)])
    qk_rot = _rope(proj, *_rope_tables(pos))
    k_rot = qk_rot[:, Q_ATTN:]
    v_new = proj[:, OFF_V:OFF_V + KV_ATTN]
    o_a_p = _attn_prefill(qk_rot, proj, B, S)
    grp = ATTN_HEADS // KV_HEADS
    q_s = qk_rot[Tp:, :Q_ATTN].reshape(Bs, ATTN_HEADS, HEAD_DIM)
    kx = jnp.repeat(k_rot[Tp:].reshape(Bs, KV_HEADS, HEAD_DIM), grp, axis=1)
    vx = jnp.repeat(v_new[Tp:].reshape(Bs, KV_HEADS, HEAD_DIM), grp, axis=1)
    o_a_s = _attn_decode(q_s, kx, vx, cache_k.reshape(-1, HEAD_DIM), cache_v.reshape(-1, HEAD_DIM), page_table)
    o_a = jnp.concatenate([o_a_p, o_a_s.reshape(Bs, Q_ATTN)], axis=0)

    alog_pad = jnp.pad(a_log.astype(F32), ((0, 0), (0, HEAD_DIM - H)))
    dtb_pad = jnp.pad(dt_bias.astype(F32), ((0, 0), (0, HEAD_DIM - H)))
    gb = _gates(ab, alog_pad, dtb_pad)
    qkv_p = _conv_prompt(proj, conv_w[0], B, S)
    buf_t = state_conv[0].transpose(1, 0, 2)
    qkv_s = _conv_sample(proj, buf_t, conv_w[0], Tp // Bs)
    n_chunks = S // GDN_CHUNK

    def chunk_rows(a):
        return a.reshape(B, n_chunks, GDN_CHUNK, H).transpose(0, 1, 3, 2)

    o_g_p, ssm_p = _gdn_prompt(qkv_p, chunk_rows(gb[:Tp, :H]), chunk_rows(gb[:Tp, H:2 * H]), B, S)
    o_g_s, ssm_s = _gdn_sample(qkv_s.reshape(Bs, 3 * H, HEAD_DIM), gb[Tp:].reshape(Bs, 1, HEAD_DIM),
                               state_ssm[0].astype(F32))
    o_g = jnp.concatenate([o_g_p, o_g_s.reshape(Bs, V_GDN)], axis=0)
    o_gn = _gated_norm(o_g, proj, gdn_norm)

    mix = _matmul(jnp.concatenate([o_a, o_gn], axis=1), w_out[0], name="out_proj")

    x1, h2, h2_packed, logits = _mid(x_all, mix, norm_mix_post, norm_ffn_pre, mod_rows, w_router[0], rb_per_seq)
    idx_l, wts_l = _route(logits, e_bias)
    routed = _moe_routed(h2_packed, idx_l[:, :TOP_K], wts_l[:, :TOP_K], w1[0], w3[0], w2[0])
    shared = _shared_ffn(h2, ws1[0].astype(BF16), ws3[0].astype(BF16), ws2[0].astype(BF16))
    y_all = _final(x1, routed, shared, norm_ffn_post, mod_rows, rb_per_seq)

    y_prompt = y_all[:Tp].reshape(B, S, D)
    y_sample = y_all[Tp:].reshape(Bs, 1, D)
    k_prompt = k_rot[:Tp].reshape(1, B, S, KV_HEADS, HEAD_DIM)
    v_prompt = v_new[:Tp].reshape(1, B, S, KV_HEADS, HEAD_DIM)
    k_sample = k_rot[Tp:].reshape(1, Bs, 1, KV_HEADS, HEAD_DIM)
    v_sample = v_new[Tp:].reshape(1, Bs, 1, KV_HEADS, HEAD_DIM)
    def pre_conv_rows(r0, r1):
        return lax.slice(proj, (r0, OFF_CONV), (r1, OFF_CONV + GDN_CONV_CH))

    conv_prompt = jnp.stack([pre_conv_rows((b + 1) * S - (CONV_W - 1), (b + 1) * S) for b in range(B)])[None]
    conv_sample = jnp.concatenate([state_conv[0][:, 1:], pre_conv_rows(Tp, T)[:, None, :]], axis=1)[None]
    return (y_prompt, y_sample, k_prompt, v_prompt, k_sample, v_sample,
            ssm_p[None].astype(state_ssm.dtype), ssm_s[None].astype(state_ssm.dtype),
            conv_prompt.astype(state_conv.dtype), conv_sample.astype(state_conv.dtype))
```
